```python
import math
import jax
import jax.numpy as jnp
from jax import lax
import numpy as np

D_MODEL = 2048
BATCH = 1
SEQ = 16384
DEPTH = 1

DIFF_WIDTH = D_MODEL // 2
SB_WIDTH = D_MODEL - DIFF_WIDTH
DIFF_QK_DIM = 64
DIFF_V_DIM = 2 * DIFF_QK_DIM
DIFF_HEADS = DIFF_WIDTH // DIFF_V_DIM
SB_HEAD_DIM = 128
SB_HEADS = SB_WIDTH // SB_HEAD_DIM
D_FF = 4 * D_MODEL
ROPE_THETA = 500000.0
ROPE_FRACTION_DEN = 4
BLOCK_Q = 128
NORM_EPS = 1e-6
NEG_INF = -1e30
PROJ_SPLITS = (
    2 * DIFF_HEADS * DIFF_QK_DIM,
    2 * DIFF_HEADS * DIFF_QK_DIM,
    DIFF_HEADS * DIFF_V_DIM,
    SB_HEADS * SB_HEAD_DIM,
    SB_HEADS * SB_HEAD_DIM,
    SB_HEADS * SB_HEAD_DIM,
)
PROJ_WIDTH = sum(PROJ_SPLITS)

kernel_name = "hymba_diff_stickbreaking_hybrid"


def rmsnorm(x, g):
    xf = x.astype(jnp.float32)
    y = xf * lax.rsqrt(jnp.mean(jnp.square(xf), axis=-1, keepdims=True) + NORM_EPS)
    return (y * g.astype(jnp.float32)).astype(x.dtype)


def partial_rotary(x, pos):
    d = x.shape[-1]
    rot = d // ROPE_FRACTION_DEN
    half = rot // 2
    inv_freq = ROPE_THETA ** (-jnp.arange(0, rot, 2, dtype=jnp.float32) / rot)
    ang = pos.astype(jnp.float32)[:, None] * inv_freq[None, :]
    cos = jnp.cos(ang)[None, :, None, :]
    sin = jnp.sin(ang)[None, :, None, :]
    xr = x[..., :rot].astype(jnp.float32)
    x1, x2 = xr[..., :half], xr[..., half:]
    rotated = jnp.concatenate([x1 * cos - x2 * sin, x2 * cos + x1 * sin], axis=-1)
    return jnp.concatenate([rotated.astype(x.dtype), x[..., rot:]], axis=-1)


def to_blocks(a):
    b, s = a.shape[0], a.shape[1]
    return jnp.swapaxes(a.reshape(b, s // BLOCK_Q, BLOCK_Q, *a.shape[2:]), 0, 1)


def from_blocks(a):
    a = jnp.swapaxes(a, 0, 1)
    return a.reshape(a.shape[0], a.shape[1] * a.shape[2], *a.shape[3:])


def differential_attention(q1, q2, k1, k2, v, lam):
    s_len = q1.shape[1]
    key_pos = jnp.arange(s_len)
    scale = DIFF_QK_DIM ** -0.5

    def block(args):
        qb1, qb2, qpos = args
        causal = key_pos[None, :] <= qpos[:, None]

        def probs(qb, k):
            sc = jnp.einsum("bqhd,bkhd->bhqk", qb, k,
                            preferred_element_type=jnp.float32) * scale
            return jax.nn.softmax(jnp.where(causal, sc, NEG_INF), axis=-1)

        w = probs(qb1, k1) - lam * probs(qb2, k2)
        return jnp.einsum("bhqk,bkhd->bqhd", w.astype(v.dtype), v)

    qpos = jnp.arange(s_len).reshape(-1, BLOCK_Q)
    out = lax.map(block, (to_blocks(q1), to_blocks(q2), qpos))
    return from_blocks(out)


def stick_breaking_attention(q, k, v):
    s_len = q.shape[1]
    key_pos = jnp.arange(s_len)
    scale = SB_HEAD_DIM ** -0.5

    def block(args):
        qb, qpos = args
        strict = key_pos[None, :] < qpos[:, None]
        z = jnp.einsum("bqhd,bkhd->bhqk", qb, k,
                       preferred_element_type=jnp.float32) * scale
        log_beta = jax.nn.log_sigmoid(z)
        log_one_minus = jnp.where(strict, jax.nn.log_sigmoid(-z), 0.0)
        log_stick = lax.cumsum(log_one_minus, axis=3, reverse=True) - log_one_minus
        a = jnp.where(strict, jnp.exp(log_beta + log_stick), 0.0)
        return jnp.einsum("bhqk,bkhd->bqhd", a.astype(v.dtype), v)

    qpos = jnp.arange(s_len).reshape(-1, BLOCK_Q)
    out = lax.map(block, (to_blocks(q), qpos))
    return from_blocks(out)


def setup_inputs(seed: int = 0) -> dict:
    key = jax.random.key(seed)
    ks = jax.random.split(key, 16)
    f32 = jnp.float32

    def normal(k, shape, scale):
        return jax.random.normal(k, shape, f32) * scale

    return {
        "x": normal(ks[0], (BATCH, SEQ, D_MODEL), 1.0),
        "ln1": 1.0 + normal(ks[1], (DEPTH, D_MODEL), 0.02),
        "w_in": normal(ks[2], (DEPTH, D_MODEL, PROJ_WIDTH), D_MODEL ** -0.5),
        "lambda_q1": normal(ks[3], (DEPTH, DIFF_QK_DIM), 0.1),
        "lambda_k1": normal(ks[4], (DEPTH, DIFF_QK_DIM), 0.1),
        "lambda_q2": normal(ks[5], (DEPTH, DIFF_QK_DIM), 0.1),
        "lambda_k2": normal(ks[6], (DEPTH, DIFF_QK_DIM), 0.1),
        "diff_head_norm": 1.0 + normal(ks[7], (DEPTH, DIFF_V_DIM), 0.02),
        "sb_head_norm": 1.0 + normal(ks[8], (DEPTH, SB_HEAD_DIM), 0.02),
        "w_out": normal(ks[9], (DEPTH, DIFF_WIDTH + SB_WIDTH, D_MODEL), (DIFF_WIDTH + SB_WIDTH) ** -0.5),
        "ln2": 1.0 + normal(ks[10], (DEPTH, D_MODEL), 0.02),
        "w_mlp_in": normal(ks[11], (DEPTH, D_MODEL, D_FF), D_MODEL ** -0.5),
        "w_mlp_out": normal(ks[12], (DEPTH, D_FF, D_MODEL), 0.2 * D_FF ** -0.5),
        "ln_f": 1.0 + normal(ks[13], (D_MODEL,), 0.02),
    }


def reference(x, ln1, w_in, lambda_q1, lambda_k1, lambda_q2, lambda_k2,
              diff_head_norm, sb_head_norm, w_out, ln2, w_mlp_in, w_mlp_out, ln_f):
    b, s_len, _ = x.shape
    pos = jnp.arange(s_len)
    split_points = list(np.cumsum(PROJ_SPLITS)[:-1])
    for l in range(DEPTH):
        lam_init = 0.8 - 0.6 * math.exp(-0.3 * l)

        h = rmsnorm(x, ln1[l])
        proj = h @ w_in[l]
        dq, dk, dv, sq, sk, sv = jnp.split(proj, split_points, axis=-1)

        dq = dq.reshape(b, s_len, DIFF_HEADS, 2, DIFF_QK_DIM)
        dk = dk.reshape(b, s_len, DIFF_HEADS, 2, DIFF_QK_DIM)
        q1 = partial_rotary(dq[..., 0, :], pos)
        q2 = partial_rotary(dq[..., 1, :], pos)
        k1 = partial_rotary(dk[..., 0, :], pos)
        k2 = partial_rotary(dk[..., 1, :], pos)
        dv = dv.reshape(b, s_len, DIFF_HEADS, DIFF_V_DIM)
        lam = (jnp.exp(jnp.sum(lambda_q1[l].astype(jnp.float32) * lambda_k1[l].astype(jnp.float32)))
               - jnp.exp(jnp.sum(lambda_q2[l].astype(jnp.float32) * lambda_k2[l].astype(jnp.float32)))
               + lam_init)
        diff_out = differential_attention(q1, q2, k1, k2, dv, lam)
        diff_out = rmsnorm(diff_out, diff_head_norm[l]) * (1.0 - lam_init)

        sq = sq.reshape(b, s_len, SB_HEADS, SB_HEAD_DIM)
        sk = sk.reshape(b, s_len, SB_HEADS, SB_HEAD_DIM)
        sv = sv.reshape(b, s_len, SB_HEADS, SB_HEAD_DIM)
        sb_out = rmsnorm(stick_breaking_attention(sq, sk, sv), sb_head_norm[l])

        mixed = jnp.concatenate([diff_out.reshape(b, s_len, DIFF_WIDTH),
                                 sb_out.reshape(b, s_len, SB_WIDTH)], axis=-1)
        x = x + mixed @ w_out[l]

        h = rmsnorm(x, ln2[l])
        x = x + jnp.square(jax.nn.relu(h @ w_mlp_in[l])) @ w_mlp_out[l]
    return rmsnorm(x, ln_f)
```

```python
import functools
import math

import jax
import jax.numpy as jnp
from jax import lax
from jax.experimental import pallas as pl
from jax.experimental.pallas import tpu as pltpu

F32 = jnp.float32
BF16 = jnp.bfloat16

D_MODEL = 2048
HEAD = 128
N_HEADS = 8
QK_DIM = 64
D_FF = 4 * D_MODEL
PROJ_WIDTH = 6 * N_HEADS * HEAD
ROT_COLS = 2 * N_HEADS * HEAD
ROT_DIMS = QK_DIM // 4
ROPE_THETA = 500000.0
NORM_EPS = 1e-6
NEG_INF = -1e30
LAM_INIT = 0.8 - 0.6 * math.exp(-0.3 * 0)
SB_DEAD_LOG = -105.0

VMEM_LIMIT = 56 * 1024 * 1024


def _cparams(n_axes):
    return pltpu.CompilerParams(dimension_semantics=("arbitrary",) * n_axes,
                                vmem_limit_bytes=VMEM_LIMIT)


def _rms(x):
    return x * lax.rsqrt(jnp.mean(x * x, axis=-1, keepdims=True) + NORM_EPS)


def _proj_kernel(x_ref, g_ref, w_ref, cos_ref, sup_ref, sdn_ref, o_ref, h_ref, *, tn, n_rot_tiles):
    n = pl.program_id(1)

    @pl.when(n == 0)
    def _():
        h_ref[...] = (_rms(x_ref[...]) * g_ref[...]).astype(BF16)

    acc = jnp.dot(h_ref[...], w_ref[...], preferred_element_type=F32)

    @pl.when(n < n_rot_tiles)
    def _():
        cos, sup, sdn = cos_ref[...], sup_ref[...], sdn_ref[...]
        for c in range(tn // HEAD):
            xc = acc[:, c * HEAD:(c + 1) * HEAD]
            y = xc * cos + pltpu.roll(xc, ROT_DIMS // 2, 1) * sup + pltpu.roll(xc, HEAD - ROT_DIMS // 2, 1) * sdn
            o_ref[c] = y.astype(BF16)

    @pl.when(n >= n_rot_tiles)
    def _():
        for c in range(tn // HEAD):
            o_ref[c] = acc[:, c * HEAD:(c + 1) * HEAD].astype(BF16)


def _rotary_tables(s_len):
    half = ROT_DIMS // 2
    inv_freq = ROPE_THETA ** (-jnp.arange(0, ROT_DIMS, 2, dtype=F32) / ROT_DIMS)
    ang = jnp.arange(s_len, dtype=F32)[:, None] * inv_freq[None, :]
    cos, sin = jnp.cos(ang), jnp.sin(ang)
    ones = jnp.ones((s_len, QK_DIM - ROT_DIMS), F32)
    zeros = jnp.zeros((s_len, QK_DIM - ROT_DIMS), F32)
    zh = jnp.zeros((s_len, half), F32)
    cos64 = jnp.concatenate([cos, cos, ones], axis=1)
    sup64 = jnp.concatenate([zh, sin, zeros], axis=1)
    sdn64 = jnp.concatenate([-sin, zh, zeros], axis=1)
    tile2 = lambda t: jnp.concatenate([t, t], axis=1)
    return tile2(cos64), tile2(sup64), tile2(sdn64)


def _project(x2, ln1, w_in_bf16, tm=512, tn=1024):
    s_len = x2.shape[0]
    cos, sup, sdn = _rotary_tables(s_len)
    kern = functools.partial(_proj_kernel, tn=tn, n_rot_tiles=ROT_COLS // tn)
    return pl.pallas_call(
        kern,
        grid=(s_len // tm, PROJ_WIDTH // tn),
        in_specs=[
            pl.BlockSpec((tm, D_MODEL), lambda m, n: (m, 0)),
            pl.BlockSpec((1, D_MODEL), lambda m, n: (0, 0)),
            pl.BlockSpec((D_MODEL, tn), lambda m, n: (0, n)),
            pl.BlockSpec((tm, HEAD), lambda m, n: (m, 0)),
            pl.BlockSpec((tm, HEAD), lambda m, n: (m, 0)),
            pl.BlockSpec((tm, HEAD), lambda m, n: (m, 0)),
        ],
        out_specs=pl.BlockSpec((tn // HEAD, tm, HEAD), lambda m, n: (n, m, 0)),
        out_shape=jax.ShapeDtypeStruct((PROJ_WIDTH // HEAD, s_len, HEAD), BF16),
        scratch_shapes=[pltpu.VMEM((tm, D_MODEL), BF16)],
        compiler_params=_cparams(2),
        name="proj_in",
    )(x2, ln1.reshape(1, D_MODEL), w_in_bf16, cos, sup, sdn)


def _diff_kernel(lq1_ref, lk1_ref, lq2_ref, lk2_ref, q_ref, k_ref, v_ref, g_ref, o_ref,
                 qq_ref, m_ref, l_ref, acc_ref, *, blk):
    i = pl.program_id(1)
    q = q_ref[0]
    lane = lax.broadcasted_iota(jnp.int32, (blk, HEAD), 1)
    zero = jnp.zeros_like(q)
    qq_ref[0:blk, :] = jnp.where(lane < QK_DIM, q, zero)
    qq_ref[blk:2 * blk, :] = jnp.where(lane >= QK_DIM, q, zero)
    m_ref[...] = jnp.full(m_ref.shape, NEG_INF, F32)
    l_ref[...] = jnp.zeros(l_ref.shape, F32)
    acc_ref[...] = jnp.zeros(acc_ref.shape, F32)
    scale = QK_DIM ** -0.5

    def step(j, masked):
        start = pl.multiple_of(j * blk, blk)
        kj = k_ref[0, pl.ds(start, blk), :]
        vj = v_ref[0, pl.ds(start, blk), :]
        s = lax.dot_general(qq_ref[...], kj, (((1,), (1,)), ((), ())),
                            preferred_element_type=F32) * scale
        if masked:
            row = lax.broadcasted_iota(jnp.int32, s.shape, 0)
            col = lax.broadcasted_iota(jnp.int32, s.shape, 1)
            row = jnp.where(row >= blk, row - blk, row)
            s = jnp.where(col <= row, s, NEG_INF)
        m_prev = m_ref[...]
        m_next = jnp.maximum(m_prev, jnp.max(s, axis=1, keepdims=True))
        alpha = jnp.exp(m_prev - m_next)
        p = jnp.exp(s - m_next[:, 0:1])
        l_ref[...] = alpha * l_ref[...] + jnp.sum(p, axis=1, keepdims=True)
        acc_ref[...] = alpha * acc_ref[...] + jnp.dot(p.astype(BF16), vj, preferred_element_type=F32)
        m_ref[...] = m_next

    def body(j, carry):
        step(j, False)
        return carry

    lax.fori_loop(0, i, body, 0)
    step(i, True)

    lam = (jnp.exp(jnp.sum(lq1_ref[...] * lk1_ref[...], axis=-1, keepdims=True))
           - jnp.exp(jnp.sum(lq2_ref[...] * lk2_ref[...], axis=-1, keepdims=True)) + LAM_INIT)
    o = acc_ref[...] / l_ref[...]
    d = o[0:blk] - lam * o[blk:2 * blk]
    o_ref[...] = (_rms(d) * g_ref[...] * (1.0 - LAM_INIT)).astype(BF16)


def _diff_attention(proj, lq1, lk1, lq2, lk2, g, blk=256):
    s_len = proj.shape[1]
    vec = lambda a: a.reshape(1, -1).astype(F32)
    small = lambda width: pl.BlockSpec((1, width), lambda h, i: (0, 0))
    return pl.pallas_call(
        functools.partial(_diff_kernel, blk=blk),
        grid=(N_HEADS, s_len // blk),
        in_specs=[
            small(QK_DIM), small(QK_DIM), small(QK_DIM), small(QK_DIM),
            pl.BlockSpec((1, blk, HEAD), lambda h, i: (h, i, 0)),
            pl.BlockSpec((1, s_len, HEAD), lambda h, i: (N_HEADS + h, 0, 0)),
            pl.BlockSpec((1, s_len, HEAD), lambda h, i: (2 * N_HEADS + h, 0, 0)),
            small(HEAD),
        ],
        out_specs=pl.BlockSpec((blk, HEAD), lambda h, i: (i, h)),
        out_shape=jax.ShapeDtypeStruct((s_len, N_HEADS * HEAD), BF16),
        scratch_shapes=[
            pltpu.VMEM((2 * blk, HEAD), BF16),
            pltpu.VMEM((2 * blk, HEAD), F32),
            pltpu.VMEM((2 * blk, HEAD), F32),
            pltpu.VMEM((2 * blk, HEAD), F32),
        ],
        compiler_params=_cparams(2),
        name="diff_attn",
    )(vec(lq1), vec(lk1), vec(lq2), vec(lk2), proj, proj, proj, vec(g))


def _sb_kernel(q_ref, k_ref, v_ref, g_ref, o_ref, acc_ref, stick_ref, *, blk):
    i = pl.program_id(1)
    q = q_ref[0]
    acc_ref[...] = jnp.zeros(acc_ref.shape, F32)
    stick_ref[...] = jnp.zeros(stick_ref.shape, F32)
    scale = HEAD ** -0.5
    r = lax.broadcasted_iota(jnp.int32, (blk, blk), 0)
    c = lax.broadcasted_iota(jnp.int32, (blk, blk), 1)
    later = jnp.where(r > c, 1.0, 0.0).astype(BF16)
    strict = c < r

    def block(j, masked):
        start = pl.multiple_of(j * blk, blk)
        kj = k_ref[0, pl.ds(start, blk), :]
        vj = v_ref[0, pl.ds(start, blk), :]
        z = lax.dot_general(q, kj, (((1,), (1,)), ((), ())), preferred_element_type=F32) * scale
        log_beta = jnp.minimum(z, 0.0) - jnp.log(1.0 + jnp.exp(-jnp.abs(z)))
        log_rest = log_beta - z
        if masked:
            log_rest = jnp.where(strict, log_rest, 0.0)
        hi = log_rest.astype(BF16)
        lo = (log_rest - hi.astype(F32)).astype(BF16)
        within = (jnp.dot(hi, later, preferred_element_type=F32)
                  + jnp.dot(lo, later, preferred_element_type=F32))
        stick = stick_ref[...]
        a = jnp.exp(log_beta + within + stick[:, 0:1])
        if masked:
            a = jnp.where(strict, a, 0.0)
        acc_ref[...] += jnp.dot(a.astype(BF16), vj, preferred_element_type=F32)
        stick_ref[...] = stick + jnp.sum(log_rest, axis=1, keepdims=True)

    def alive():
        return jnp.max(stick_ref[...]) > SB_DEAD_LOG

    block(i, True)

    def cond(state):
        j, go = state
        return jnp.logical_and(j >= 0, go)

    def body(state):
        j, _ = state
        block(j, False)
        return j - 1, alive()

    lax.while_loop(cond, body, (i - 1, alive()))
    o_ref[...] = (_rms(acc_ref[...]) * g_ref[...]).astype(BF16)


def _sb_attention(proj, g, blk=256):
    s_len = proj.shape[1]
    return pl.pallas_call(
        functools.partial(_sb_kernel, blk=blk),
        grid=(N_HEADS, s_len // blk),
        in_specs=[
            pl.BlockSpec((1, blk, HEAD), lambda h, i: (3 * N_HEADS + h, i, 0)),
            pl.BlockSpec((1, s_len, HEAD), lambda h, i: (4 * N_HEADS + h, 0, 0)),
            pl.BlockSpec((1, s_len, HEAD), lambda h, i: (5 * N_HEADS + h, 0, 0)),
            pl.BlockSpec((1, HEAD), lambda h, i: (0, 0)),
        ],
        out_specs=pl.BlockSpec((blk, HEAD), lambda h, i: (i, h)),
        out_shape=jax.ShapeDtypeStruct((s_len, N_HEADS * HEAD), BF16),
        scratch_shapes=[
            pltpu.VMEM((blk, HEAD), F32),
            pltpu.VMEM((blk, HEAD), F32),
        ],
        compiler_params=_cparams(2),
        name="sb_attn",
    )(proj, proj, proj, g.reshape(1, HEAD).astype(F32))


def _outproj_kernel(md_ref, ms_ref, x_ref, w_ref, g_ref, x1_ref, h2_ref):
    half = md_ref.shape[1]
    y = (jnp.dot(md_ref[...], w_ref[0:half, :], preferred_element_type=F32)
         + jnp.dot(ms_ref[...], w_ref[half:2 * half, :], preferred_element_type=F32))
    x1 = x_ref[...] + y
    x1_ref[...] = x1
    h2_ref[...] = (_rms(x1) * g_ref[...]).astype(BF16)


def _out_project(mixed_diff, mixed_sb, x2, w_out_bf16, ln2, tm=512):
    s_len = x2.shape[0]
    half = N_HEADS * HEAD
    return pl.pallas_call(
        _outproj_kernel,
        grid=(s_len // tm,),
        in_specs=[
            pl.BlockSpec((tm, half), lambda m: (m, 0)),
            pl.BlockSpec((tm, half), lambda m: (m, 0)),
            pl.BlockSpec((tm, D_MODEL), lambda m: (m, 0)),
            pl.BlockSpec((2 * half, D_MODEL), lambda m: (0, 0)),
            pl.BlockSpec((1, D_MODEL), lambda m: (0, 0)),
        ],
        out_specs=[
            pl.BlockSpec((tm, D_MODEL), lambda m: (m, 0)),
            pl.BlockSpec((tm, D_MODEL), lambda m: (m, 0)),
        ],
        out_shape=[
            jax.ShapeDtypeStruct((s_len, D_MODEL), F32),
            jax.ShapeDtypeStruct((s_len, D_MODEL), BF16),
        ],
        compiler_params=_cparams(1),
        name="proj_out",
    )(mixed_diff, mixed_sb, x2, w_out_bf16, ln2.reshape(1, D_MODEL))


def _mlp_kernel(h_ref, w1_ref, w2_ref, x1_ref, g_ref, o_ref, acc_ref):
    f = pl.program_id(1)
    a = jnp.dot(h_ref[...], w1_ref[...], preferred_element_type=F32)
    a = jnp.square(jnp.maximum(a, 0.0)).astype(BF16)
    y = jnp.dot(a, w2_ref[...], preferred_element_type=F32)

    @pl.when(f == 0)
    def _():
        acc_ref[...] = x1_ref[...] + y

    @pl.when(f > 0)
    def _():
        acc_ref[...] += y

    @pl.when(f == pl.num_programs(1) - 1)
    def _():
        o_ref[...] = _rms(acc_ref[...]) * g_ref[...]


def _mlp(h2, x1, w1_bf16, w2_bf16, ln_f, tm=512, tf=1024):
    s_len = h2.shape[0]
    return pl.pallas_call(
        _mlp_kernel,
        grid=(s_len // tm, D_FF // tf),
        in_specs=[
            pl.BlockSpec((tm, D_MODEL), lambda m, f: (m, 0)),
            pl.BlockSpec((D_MODEL, tf), lambda m, f: (0, f)),
            pl.BlockSpec((tf, D_MODEL), lambda m, f: (f, 0)),
            pl.BlockSpec((tm, D_MODEL), lambda m, f: (m, 0)),
            pl.BlockSpec((1, D_MODEL), lambda m, f: (0, 0)),
        ],
        out_specs=pl.BlockSpec((tm, D_MODEL), lambda m, f: (m, 0)),
        out_shape=jax.ShapeDtypeStruct((s_len, D_MODEL), F32),
        scratch_shapes=[pltpu.VMEM((tm, D_MODEL), F32)],
        compiler_params=_cparams(2),
        name="mlp",
    )(h2, w1_bf16, w2_bf16, x1, ln_f.reshape(1, D_MODEL))


def kernel(x, ln1, w_in, lambda_q1, lambda_k1, lambda_q2, lambda_k2, diff_head_norm, sb_head_norm,
           w_out, ln2, w_mlp_in, w_mlp_out, ln_f):
    b, s_len, _ = x.shape
    assert b == 1 and ln1.shape[0] == 1
    x2 = x.reshape(s_len, D_MODEL)
    proj = _project(x2, ln1[0], w_in[0].astype(BF16))
    mixed_diff = _diff_attention(proj, lambda_q1[0], lambda_k1[0], lambda_q2[0], lambda_k2[0], diff_head_norm[0])
    mixed_sb = _sb_attention(proj, sb_head_norm[0])
    x1, h2 = _out_project(mixed_diff, mixed_sb, x2, w_out[0].astype(BF16), ln2[0])
    out = _mlp(h2, x1, w_mlp_in[0].astype(BF16), w_mlp_out[0].astype(BF16), ln_f)
    return out.reshape(b, s_len, D_MODEL)
```

```python
import functools
import math

import jax
import jax.numpy as jnp
from jax import lax
from jax.experimental import pallas as pl
from jax.experimental.pallas import tpu as pltpu

F32 = jnp.float32
BF16 = jnp.bfloat16

D_MODEL = 2048
HEAD = 128
N_HEADS = 8
QK_DIM = 64
D_FF = 4 * D_MODEL
PROJ_WIDTH = 6 * N_HEADS * HEAD
ROT_COLS = 2 * N_HEADS * HEAD
ROT_DIMS = QK_DIM // 4
ROPE_THETA = 500000.0
NORM_EPS = 1e-6
NEG_INF = -1e30
LAM_INIT = 0.8 - 0.6 * math.exp(-0.3 * 0)
SB_DEAD_LOG = -105.0
DIFF_BQ = 256
DIFF_BK = 1024
VT_BLK = 256
CHUNK = 256
LOG2E = math.log2(math.e)

VMEM_LIMIT = 56 * 1024 * 1024


def _cparams(n_axes, flags=None):
    return pltpu.CompilerParams(dimension_semantics=("arbitrary",) * n_axes,
                                vmem_limit_bytes=VMEM_LIMIT, flags=flags)


def _rms(x):
    return x * lax.rsqrt(jnp.mean(x * x, axis=-1, keepdims=True) + NORM_EPS)


def _proj_kernel(x_ref, g_ref, w_ref, cos_ref, sup_ref, sdn_ref, o_ref, qt_ref, vt_ref, h_ref, *, tn):
    n = pl.program_id(1)
    n_heads_tile = tn // HEAD
    assert n_heads_tile == N_HEADS

    @pl.when(n == 0)
    def _():
        h_ref[...] = (_rms(x_ref[...]) * g_ref[...]).astype(BF16)

    acc = jnp.dot(h_ref[...], w_ref[...], preferred_element_type=F32)

    def rotary(xc):
        return (xc * cos_ref[...] + pltpu.roll(xc, ROT_DIMS // 2, 1) * sup_ref[...]
                + pltpu.roll(xc, HEAD - ROT_DIMS // 2, 1) * sdn_ref[...])

    def store_transposed(dst_ref, c, y):
        blocks, _, width = dst_ref.shape[1:]
        for b in range(blocks):
            dst_ref[c, b] = y[b * width:(b + 1) * width, :].T.astype(BF16)

    @pl.when(n == 0)
    def _():
        for c in range(n_heads_tile):
            y = rotary(acc[:, c * HEAD:(c + 1) * HEAD])
            o_ref[c] = y.astype(BF16)
            store_transposed(qt_ref, c, y)

    @pl.when(n == 1)
    def _():
        for c in range(n_heads_tile):
            o_ref[c] = rotary(acc[:, c * HEAD:(c + 1) * HEAD]).astype(BF16)

    @pl.when(n == 2)
    def _():
        for c in range(n_heads_tile):
            y = acc[:, c * HEAD:(c + 1) * HEAD]
            o_ref[c] = y.astype(BF16)
            store_transposed(vt_ref, c, y)

    @pl.when(n > 2)
    def _():
        for c in range(n_heads_tile):
            o_ref[c] = acc[:, c * HEAD:(c + 1) * HEAD].astype(BF16)


def _rotary_tables(s_len):
    half = ROT_DIMS // 2
    inv_freq = ROPE_THETA ** (-jnp.arange(0, ROT_DIMS, 2, dtype=F32) / ROT_DIMS)
    ang = jnp.arange(s_len, dtype=F32)[:, None] * inv_freq[None, :]
    cos, sin = jnp.cos(ang), jnp.sin(ang)
    ones = jnp.ones((s_len, QK_DIM - ROT_DIMS), F32)
    zeros = jnp.zeros((s_len, QK_DIM - ROT_DIMS), F32)
    zh = jnp.zeros((s_len, half), F32)
    cos64 = jnp.concatenate([cos, cos, ones], axis=1)
    sup64 = jnp.concatenate([zh, sin, zeros], axis=1)
    sdn64 = jnp.concatenate([-sin, zh, zeros], axis=1)
    tile2 = lambda t: jnp.concatenate([t, t], axis=1)
    return tile2(cos64), tile2(sup64), tile2(sdn64)


def _project(x2, ln1, w_in_bf16, tm=512, tn=N_HEADS * HEAD):
    s_len = x2.shape[0]
    cos, sup, sdn = _rotary_tables(s_len)
    kern = functools.partial(_proj_kernel, tn=tn)
    return pl.pallas_call(
        kern,
        grid=(s_len // tm, PROJ_WIDTH // tn),
        in_specs=[
            pl.BlockSpec((tm, D_MODEL), lambda m, n: (m, 0)),
            pl.BlockSpec((1, D_MODEL), lambda m, n: (0, 0)),
            pl.BlockSpec((D_MODEL, tn), lambda m, n: (0, n)),
            pl.BlockSpec((tm, HEAD), lambda m, n: (m, 0)),
            pl.BlockSpec((tm, HEAD), lambda m, n: (m, 0)),
            pl.BlockSpec((tm, HEAD), lambda m, n: (m, 0)),
        ],
        out_specs=[
            pl.BlockSpec((tn // HEAD, tm, HEAD), lambda m, n: (n, m, 0)),
            pl.BlockSpec((N_HEADS, tm // DIFF_BQ, HEAD, DIFF_BQ), lambda m, n: (0, m, 0, 0)),
            pl.BlockSpec((N_HEADS, tm // VT_BLK, HEAD, VT_BLK), lambda m, n: (0, m, 0, 0)),
        ],
        out_shape=[
            jax.ShapeDtypeStruct((PROJ_WIDTH // HEAD, s_len, HEAD), BF16),
            jax.ShapeDtypeStruct((N_HEADS, s_len // DIFF_BQ, HEAD, DIFF_BQ), BF16),
            jax.ShapeDtypeStruct((N_HEADS, s_len // VT_BLK, HEAD, VT_BLK), BF16),
        ],
        scratch_shapes=[pltpu.VMEM((tm, D_MODEL), BF16)],
        compiler_params=_cparams(2),
        name="proj_in",
    )(x2, ln1.reshape(1, D_MODEL), w_in_bf16, cos, sup, sdn)


def _diff_kernel(lq1_ref, lk1_ref, lq2_ref, lk2_ref, qt_ref, k_ref, vt_ref, g_ref, o_ref,
                 qqt_ref, s0_ref, s1_ref, cmax0_ref, cmax1_ref, p_ref, m_ref, l_ref, acc_ref):
    bq, bk = DIFF_BQ, DIFF_BK
    i = pl.program_id(1)
    qt = qt_ref[0, 0]
    sub = lax.broadcasted_iota(jnp.int32, (HEAD, bq), 0)
    zero = jnp.zeros_like(qt)
    qqt_ref[:, 0:bq] = jnp.where(sub < QK_DIM, qt, zero)
    qqt_ref[:, bq:2 * bq] = jnp.where(sub >= QK_DIM, qt, zero)
    m_ref[...] = jnp.full(m_ref.shape, NEG_INF, F32)
    l_ref[...] = jnp.zeros(l_ref.shape, F32)
    acc_ref[...] = jnp.zeros(acc_ref.shape, F32)
    s_ref, cmax_ref = (s0_ref, s1_ref), (cmax0_ref, cmax1_ref)
    c = QK_DIM ** -0.5 * LOG2E
    n_blocks = (i * bq) // bk + 1

    def scores_chunk(j, slot, r, cmax):
        rows = pl.ds(pl.multiple_of(j * bk + r * CHUNK, CHUNK), CHUNK)
        st = jnp.dot(k_ref[0, rows, :], qqt_ref[...], preferred_element_type=F32)
        s_ref[slot][r * CHUNK:(r + 1) * CHUNK, :] = st
        return jnp.maximum(cmax, jnp.max(st, axis=0, keepdims=True))

    def softmax_chunk(slot, r, mc, psum):
        pt = jnp.exp2(s_ref[slot][r * CHUNK:(r + 1) * CHUNK, :] * c - mc)
        p_ref[r * CHUNK:(r + 1) * CHUNK, :] = pt.astype(BF16)
        return psum + jnp.sum(pt, axis=0, keepdims=True)

    def softmax_scores(par, next_block):
        m_prev = m_ref[...]
        m_next = jnp.maximum(m_prev, cmax_ref[par][...])
        alpha = jnp.exp2((m_prev - m_next) * c)
        mc = m_next * c
        psum = jnp.zeros(m_prev.shape, F32)
        cmax = jnp.full(m_prev.shape, NEG_INF, F32)
        for r in range(bk // CHUNK):
            psum = softmax_chunk(par, r, mc, psum)
            if next_block is not None:
                cmax = scores_chunk(next_block, 1 - par, r, cmax)
        if next_block is not None:
            cmax_ref[1 - par][...] = cmax
        l_ref[...] = alpha * l_ref[...] + psum
        m_ref[...] = m_next
        return alpha

    def first_scores():
        cmax = jnp.full(m_ref.shape, NEG_INF, F32)
        for r in range(bk // CHUNK):
            cmax = scores_chunk(0, 0, r, cmax)
        cmax_ref[0][...] = cmax

    def mask_diagonal(slot):
        st = s_ref[slot][...]
        key = (n_blocks - 1) * bk + lax.broadcasted_iota(jnp.int32, st.shape, 0)
        col = lax.broadcasted_iota(jnp.int32, st.shape, 1)
        qpos = i * bq + jnp.where(col >= bq, col - bq, col)
        st = jnp.where(key <= qpos, st, NEG_INF)
        s_ref[slot][...] = st
        cmax_ref[slot][...] = jnp.max(st, axis=0, keepdims=True)

    def values(j, alpha):
        first = j * (bk // VT_BLK)
        pv = sum(jnp.dot(vt_ref[0, first + t], p_ref[t * VT_BLK:(t + 1) * VT_BLK, :],
                         preferred_element_type=F32) for t in range(bk // VT_BLK))
        acc_ref[...] = alpha * acc_ref[...] + pv

    def stage(j, par):
        alpha = softmax_scores(par, j + 1)
        values(j, alpha)

    def body(j, carry):
        @pl.when(j % 2 == 0)
        def _():
            stage(j, 0)

        @pl.when(j % 2 == 1)
        def _():
            stage(j, 1)

        return carry

    def drain(par):
        mask_diagonal(par)
        alpha = softmax_scores(par, None)
        values(n_blocks - 1, alpha)

    first_scores()
    lax.fori_loop(0, n_blocks - 1, body, 0)

    @pl.when(n_blocks % 2 == 1)
    def _():
        drain(0)

    @pl.when(n_blocks % 2 == 0)
    def _():
        drain(1)

    lam = (jnp.exp(jnp.sum(lq1_ref[...] * lk1_ref[...], axis=-1, keepdims=True))
           - jnp.exp(jnp.sum(lq2_ref[...] * lk2_ref[...], axis=-1, keepdims=True)) + LAM_INIT)
    ot = acc_ref[...] / l_ref[...]
    dt = ot[:, 0:bq] - lam * ot[:, bq:2 * bq]
    yt = dt * lax.rsqrt(jnp.mean(dt * dt, axis=0, keepdims=True) + NORM_EPS)
    o_ref[...] = (yt.T * g_ref[...] * (1.0 - LAM_INIT)).astype(BF16)


def _diff_attention(proj, qt, vt, lq1, lk1, lq2, lk2, g):
    s_len = proj.shape[1]
    assert DIFF_BK % DIFF_BQ == 0 and s_len % DIFF_BK == 0
    vec = lambda a: a.reshape(1, -1).astype(F32)
    small = lambda width: pl.BlockSpec((1, width), lambda h, i: (0, 0))
    return pl.pallas_call(
        _diff_kernel,
        grid=(N_HEADS, s_len // DIFF_BQ),
        in_specs=[
            small(QK_DIM), small(QK_DIM), small(QK_DIM), small(QK_DIM),
            pl.BlockSpec((1, 1, HEAD, DIFF_BQ), lambda h, i: (h, i, 0, 0)),
            pl.BlockSpec((1, s_len, HEAD), lambda h, i: (N_HEADS + h, 0, 0)),
            pl.BlockSpec((1, s_len // VT_BLK, HEAD, VT_BLK), lambda h, i: (h, 0, 0, 0)),
            small(HEAD),
        ],
        out_specs=pl.BlockSpec((DIFF_BQ, HEAD), lambda h, i: (i, h)),
        out_shape=jax.ShapeDtypeStruct((s_len, N_HEADS * HEAD), BF16),
        scratch_shapes=[
            pltpu.VMEM((HEAD, 2 * DIFF_BQ), BF16),
            pltpu.VMEM((DIFF_BK, 2 * DIFF_BQ), F32),
            pltpu.VMEM((DIFF_BK, 2 * DIFF_BQ), F32),
            pltpu.VMEM((1, 2 * DIFF_BQ), F32),
            pltpu.VMEM((1, 2 * DIFF_BQ), F32),
            pltpu.VMEM((DIFF_BK, 2 * DIFF_BQ), BF16),
            pltpu.VMEM((1, 2 * DIFF_BQ), F32),
            pltpu.VMEM((1, 2 * DIFF_BQ), F32),
            pltpu.VMEM((HEAD, 2 * DIFF_BQ), F32),
        ],
        compiler_params=_cparams(2),
        name="diff_attn",
    )(vec(lq1), vec(lk1), vec(lq2), vec(lk2), qt, proj, vt, vec(g))


def _sb_kernel(q_ref, k_ref, v_ref, g_ref, o_ref, acc_ref, stick_ref, *, blk):
    i = pl.program_id(1)
    q = q_ref[0]
    acc_ref[...] = jnp.zeros(acc_ref.shape, F32)
    stick_ref[...] = jnp.zeros(stick_ref.shape, F32)
    scale = HEAD ** -0.5
    r = lax.broadcasted_iota(jnp.int32, (blk, blk), 0)
    c = lax.broadcasted_iota(jnp.int32, (blk, blk), 1)
    later = jnp.where(r > c, 1.0, 0.0).astype(BF16)
    strict = c < r

    def block(j, masked):
        start = pl.multiple_of(j * blk, blk)
        kj = k_ref[0, pl.ds(start, blk), :]
        vj = v_ref[0, pl.ds(start, blk), :]
        z = lax.dot_general(q, kj, (((1,), (1,)), ((), ())), preferred_element_type=F32) * scale
        log_beta = jnp.minimum(z, 0.0) - jnp.log(1.0 + jnp.exp(-jnp.abs(z)))
        log_rest = log_beta - z
        if masked:
            log_rest = jnp.where(strict, log_rest, 0.0)
        hi = log_rest.astype(BF16)
        lo = (log_rest - hi.astype(F32)).astype(BF16)
        within = (jnp.dot(hi, later, preferred_element_type=F32)
                  + jnp.dot(lo, later, preferred_element_type=F32))
        stick = stick_ref[...]
        a = jnp.exp(log_beta + within + stick[:, 0:1])
        if masked:
            a = jnp.where(strict, a, 0.0)
        acc_ref[...] += jnp.dot(a.astype(BF16), vj, preferred_element_type=F32)
        stick_ref[...] = stick + jnp.sum(log_rest, axis=1, keepdims=True)

    def alive():
        return jnp.max(stick_ref[...]) > SB_DEAD_LOG

    block(i, True)

    def cond(state):
        j, go = state
        return jnp.logical_and(j >= 0, go)

    def body(state):
        j, _ = state
        block(j, False)
        return j - 1, alive()

    lax.while_loop(cond, body, (i - 1, alive()))
    o_ref[...] = (_rms(acc_ref[...]) * g_ref[...]).astype(BF16)


def _sb_attention(proj, g, blk=256):
    s_len = proj.shape[1]
    return pl.pallas_call(
        functools.partial(_sb_kernel, blk=blk),
        grid=(N_HEADS, s_len // blk),
        in_specs=[
            pl.BlockSpec((1, blk, HEAD), lambda h, i: (3 * N_HEADS + h, i, 0)),
            pl.BlockSpec((1, s_len, HEAD), lambda h, i: (4 * N_HEADS + h, 0, 0)),
            pl.BlockSpec((1, s_len, HEAD), lambda h, i: (5 * N_HEADS + h, 0, 0)),
            pl.BlockSpec((1, HEAD), lambda h, i: (0, 0)),
        ],
        out_specs=pl.BlockSpec((blk, HEAD), lambda h, i: (i, h)),
        out_shape=jax.ShapeDtypeStruct((s_len, N_HEADS * HEAD), BF16),
        scratch_shapes=[
            pltpu.VMEM((blk, HEAD), F32),
            pltpu.VMEM((blk, HEAD), F32),
        ],
        compiler_params=_cparams(2),
        name="sb_attn",
    )(proj, proj, proj, g.reshape(1, HEAD).astype(F32))


def _outproj_kernel(md_ref, ms_ref, x_ref, w_ref, g_ref, x1_ref, h2_ref):
    half = md_ref.shape[1]
    y = (jnp.dot(md_ref[...], w_ref[0:half, :], preferred_element_type=F32)
         + jnp.dot(ms_ref[...], w_ref[half:2 * half, :], preferred_element_type=F32))
    x1 = x_ref[...] + y
    x1_ref[...] = x1
    h2_ref[...] = (_rms(x1) * g_ref[...]).astype(BF16)


def _out_project(mixed_diff, mixed_sb, x2, w_out_bf16, ln2, tm=512):
    s_len = x2.shape[0]
    half = N_HEADS * HEAD
    return pl.pallas_call(
        _outproj_kernel,
        grid=(s_len // tm,),
        in_specs=[
            pl.BlockSpec((tm, half), lambda m: (m, 0)),
            pl.BlockSpec((tm, half), lambda m: (m, 0)),
            pl.BlockSpec((tm, D_MODEL), lambda m: (m, 0)),
            pl.BlockSpec((2 * half, D_MODEL), lambda m: (0, 0)),
            pl.BlockSpec((1, D_MODEL), lambda m: (0, 0)),
        ],
        out_specs=[
            pl.BlockSpec((tm, D_MODEL), lambda m: (m, 0)),
            pl.BlockSpec((tm, D_MODEL), lambda m: (m, 0)),
        ],
        out_shape=[
            jax.ShapeDtypeStruct((s_len, D_MODEL), F32),
            jax.ShapeDtypeStruct((s_len, D_MODEL), BF16),
        ],
        compiler_params=_cparams(1),
        name="proj_out",
    )(mixed_diff, mixed_sb, x2, w_out_bf16, ln2.reshape(1, D_MODEL))


def _mlp_kernel(h_ref, w1_ref, w2_ref, x1_ref, g_ref, o_ref, acc_ref):
    f = pl.program_id(1)
    a = jnp.dot(h_ref[...], w1_ref[...], preferred_element_type=F32)
    a = jnp.square(jnp.maximum(a, 0.0)).astype(BF16)
    y = jnp.dot(a, w2_ref[...], preferred_element_type=F32)

    @pl.when(f == 0)
    def _():
        acc_ref[...] = x1_ref[...] + y

    @pl.when(f > 0)
    def _():
        acc_ref[...] += y

    @pl.when(f == pl.num_programs(1) - 1)
    def _():
        o_ref[...] = _rms(acc_ref[...]) * g_ref[...]


def _mlp(h2, x1, w1_bf16, w2_bf16, ln_f, tm=512, tf=1024):
    s_len = h2.shape[0]
    return pl.pallas_call(
        _mlp_kernel,
        grid=(s_len // tm, D_FF // tf),
        in_specs=[
            pl.BlockSpec((tm, D_MODEL), lambda m, f: (m, 0)),
            pl.BlockSpec((D_MODEL, tf), lambda m, f: (0, f)),
            pl.BlockSpec((tf, D_MODEL), lambda m, f: (f, 0)),
            pl.BlockSpec((tm, D_MODEL), lambda m, f: (m, 0)),
            pl.BlockSpec((1, D_MODEL), lambda m, f: (0, 0)),
        ],
        out_specs=pl.BlockSpec((tm, D_MODEL), lambda m, f: (m, 0)),
        out_shape=jax.ShapeDtypeStruct((s_len, D_MODEL), F32),
        scratch_shapes=[pltpu.VMEM((tm, D_MODEL), F32)],
        compiler_params=_cparams(2),
        name="mlp",
    )(h2, w1_bf16, w2_bf16, x1, ln_f.reshape(1, D_MODEL))


def kernel(x, ln1, w_in, lambda_q1, lambda_k1, lambda_q2, lambda_k2, diff_head_norm, sb_head_norm,
           w_out, ln2, w_mlp_in, w_mlp_out, ln_f):
    b, s_len, _ = x.shape
    assert b == 1 and ln1.shape[0] == 1
    x2 = x.reshape(s_len, D_MODEL)
    proj, qt, vt = _project(x2, ln1[0], w_in[0].astype(BF16))
    mixed_diff = _diff_attention(proj, qt, vt, lambda_q1[0], lambda_k1[0], lambda_q2[0], lambda_k2[0], diff_head_norm[0])
    mixed_sb = _sb_attention(proj, sb_head_norm[0])
    x1, h2 = _out_project(mixed_diff, mixed_sb, x2, w_out[0].astype(BF16), ln2[0])
    out = _mlp(h2, x1, w_mlp_in[0].astype(BF16), w_mlp_out[0].astype(BF16), ln_f)
    return out.reshape(b, s_len, D_MODEL)
```

```python
import functools
import math

import jax
import jax.numpy as jnp
import numpy as np
from jax import lax
from jax.experimental import pallas as pl
from jax.experimental.pallas import tpu as pltpu

F32 = jnp.float32
BF16 = jnp.bfloat16

D_MODEL = 2048
HEAD = 128
N_HEADS = 8
QK_DIM = 64
D_FF = 4 * D_MODEL
PROJ_WIDTH = 6 * N_HEADS * HEAD
ROT_COLS = 2 * N_HEADS * HEAD
ROT_DIMS = QK_DIM // 4
ROPE_THETA = 500000.0
NORM_EPS = 1e-6
NEG_INF = -1e30
LAM_INIT = 0.8 - 0.6 * math.exp(-0.3 * 0)
SB_DEAD_LOG = -105.0
DIFF_BQ = 512
DIFF_BK = 1024
CHUNK = 256
LOG2E = math.log2(math.e)
DIFF_QSCALE = QK_DIM ** -0.5 * LOG2E

VMEM_LIMIT = 56 * 1024 * 1024


def _cparams(n_axes, flags=None):
    return pltpu.CompilerParams(dimension_semantics=("arbitrary",) * n_axes,
                                vmem_limit_bytes=VMEM_LIMIT, flags=flags)


def _rms(x):
    return x * lax.rsqrt(jnp.mean(x * x, axis=-1, keepdims=True) + NORM_EPS)


def _proj_kernel(x_ref, g_ref, w_ref, cos_ref, sup_ref, sdn_ref, o_ref, qt_ref, vt_ref, h_ref, *, tn):
    n = pl.program_id(1)
    n_heads_tile = tn // HEAD
    assert n_heads_tile == N_HEADS

    @pl.when(n == 0)
    def _():
        h_ref[...] = (_rms(x_ref[...]) * g_ref[...]).astype(BF16)

    acc = jnp.dot(h_ref[...], w_ref[...], preferred_element_type=F32)

    def rotary(xc):
        return (xc * cos_ref[...] + pltpu.roll(xc, ROT_DIMS // 2, 1) * sup_ref[...]
                + pltpu.roll(xc, HEAD - ROT_DIMS // 2, 1) * sdn_ref[...])

    def store_transposed(dst_ref, c, y):
        blocks, _, width = dst_ref.shape[1:]
        for b in range(blocks):
            dst_ref[c, b] = y[b * width:(b + 1) * width, :].T.astype(BF16)

    @pl.when(n == 0)
    def _():
        for c in range(n_heads_tile):
            y = rotary(acc[:, c * HEAD:(c + 1) * HEAD])
            o_ref[c] = y.astype(BF16)
            store_transposed(qt_ref, c, y * DIFF_QSCALE)

    @pl.when(n == 1)
    def _():
        for c in range(n_heads_tile):
            o_ref[c] = rotary(acc[:, c * HEAD:(c + 1) * HEAD]).astype(BF16)

    @pl.when(n == 2)
    def _():
        for c in range(n_heads_tile):
            y = acc[:, c * HEAD:(c + 1) * HEAD]
            o_ref[c] = y.astype(BF16)
            store_transposed(vt_ref, c, y)

    @pl.when(n > 2)
    def _():
        for c in range(n_heads_tile):
            o_ref[c] = acc[:, c * HEAD:(c + 1) * HEAD].astype(BF16)


def _rotary_tables(s_len):
    half = ROT_DIMS // 2
    inv_freq = ROPE_THETA ** (-jnp.arange(0, ROT_DIMS, 2, dtype=F32) / ROT_DIMS)
    comp = np.arange(HEAD) % QK_DIM
    first, second = comp < half, (comp >= half) & (comp < ROT_DIMS)
    ang = jnp.arange(s_len, dtype=F32)[:, None] * inv_freq[comp % half][None, :]
    cos, sin = jnp.cos(ang), jnp.sin(ang)
    cos_t = jnp.where(first | second, cos, 1.0)
    sup_t = jnp.where(second, sin, 0.0)
    sdn_t = jnp.where(first, -sin, 0.0)
    return cos_t, sup_t, sdn_t


def _project(x2, ln1, w_in_bf16, tm=512, tn=N_HEADS * HEAD):
    s_len = x2.shape[0]
    cos, sup, sdn = _rotary_tables(s_len)
    kern = functools.partial(_proj_kernel, tn=tn)
    return pl.pallas_call(
        kern,
        grid=(s_len // tm, PROJ_WIDTH // tn),
        in_specs=[
            pl.BlockSpec((tm, D_MODEL), lambda m, n: (m, 0)),
            pl.BlockSpec((1, D_MODEL), lambda m, n: (0, 0)),
            pl.BlockSpec((D_MODEL, tn), lambda m, n: (0, n)),
            pl.BlockSpec((tm, HEAD), lambda m, n: (m, 0)),
            pl.BlockSpec((tm, HEAD), lambda m, n: (m, 0)),
            pl.BlockSpec((tm, HEAD), lambda m, n: (m, 0)),
        ],
        out_specs=[
            pl.BlockSpec((tn // HEAD, tm, HEAD), lambda m, n: (n, m, 0)),
            pl.BlockSpec((N_HEADS, tm // DIFF_BQ, HEAD, DIFF_BQ), lambda m, n: (0, m, 0, 0)),
            pl.BlockSpec((N_HEADS, tm // CHUNK, HEAD, CHUNK), lambda m, n: (0, m, 0, 0)),
        ],
        out_shape=[
            jax.ShapeDtypeStruct((PROJ_WIDTH // HEAD, s_len, HEAD), BF16),
            jax.ShapeDtypeStruct((N_HEADS, s_len // DIFF_BQ, HEAD, DIFF_BQ), BF16),
            jax.ShapeDtypeStruct((N_HEADS, s_len // CHUNK, HEAD, CHUNK), BF16),
        ],
        scratch_shapes=[pltpu.VMEM((tm, D_MODEL), BF16)],
        compiler_params=_cparams(2),
        name="proj_in",
    )(x2, ln1.reshape(1, D_MODEL), w_in_bf16, cos, sup, sdn)


def _diff_kernel(lq1_ref, lk1_ref, lq2_ref, lk2_ref, qt_ref, k_ref, vt_ref, g_ref, o_ref,
                 qqt_ref, s0_ref, s1_ref, cmax0_ref, cmax1_ref, p_ref, m_ref, l_ref, acc_ref):
    bq, bk = DIFF_BQ, DIFF_BK
    i = pl.program_id(1)
    qt = qt_ref[0, 0]
    sub = lax.broadcasted_iota(jnp.int32, (HEAD, bq), 0)
    zero = jnp.zeros_like(qt)
    qqt_ref[:, 0:bq] = jnp.where(sub < QK_DIM, qt, zero)
    qqt_ref[:, bq:2 * bq] = jnp.where(sub >= QK_DIM, qt, zero)
    m_ref[...] = jnp.full(m_ref.shape, NEG_INF, F32)
    l_ref[...] = jnp.zeros(l_ref.shape, F32)
    acc_ref[...] = jnp.zeros(acc_ref.shape, F32)
    s_ref, cmax_ref = (s0_ref, s1_ref), (cmax0_ref, cmax1_ref)
    n_blocks = (i * bq) // bk + 1

    def scores_chunk(j, slot, r, cmax):
        rows = pl.ds(pl.multiple_of(j * bk + r * CHUNK, CHUNK), CHUNK)
        st = jnp.dot(k_ref[0, rows, :], qqt_ref[...], preferred_element_type=F32)
        s_ref[slot][r * CHUNK:(r + 1) * CHUNK, :] = st
        return jnp.maximum(cmax, jnp.max(st, axis=0, keepdims=True))

    def softmax_chunk(slot, r, m_next, psum):
        pt = jnp.exp2(s_ref[slot][r * CHUNK:(r + 1) * CHUNK, :] - m_next)
        p_ref[r * CHUNK:(r + 1) * CHUNK, :] = pt.astype(BF16)
        return psum + jnp.sum(pt, axis=0, keepdims=True)

    def values_chunk(j, r, pv):
        d = jnp.dot(vt_ref[0, j * (bk // CHUNK) + r], p_ref[r * CHUNK:(r + 1) * CHUNK, :],
                    preferred_element_type=F32)
        return d if pv is None else pv + d

    def stage(j, par, next_block):
        m_prev = m_ref[...]
        m_next = jnp.maximum(m_prev, cmax_ref[par][...])
        alpha = jnp.exp2(m_prev - m_next)
        psum = jnp.zeros(m_prev.shape, F32)
        cmax = jnp.full(m_prev.shape, NEG_INF, F32)
        pv = None
        for r in range(bk // CHUNK):
            if next_block is not None:
                cmax = scores_chunk(next_block, 1 - par, r, cmax)
            psum = softmax_chunk(par, r, m_next, psum)
            pv = values_chunk(j, r, pv)
        if next_block is not None:
            cmax_ref[1 - par][...] = cmax
        l_ref[...] = alpha * l_ref[...] + psum
        m_ref[...] = m_next
        acc_ref[...] = alpha * acc_ref[...] + pv

    def first_scores():
        cmax = jnp.full(m_ref.shape, NEG_INF, F32)
        for r in range(bk // CHUNK):
            cmax = scores_chunk(0, 0, r, cmax)
        cmax_ref[0][...] = cmax

    def mask_diagonal(slot):
        st = s_ref[slot][...]
        key = (n_blocks - 1) * bk + lax.broadcasted_iota(jnp.int32, st.shape, 0)
        col = lax.broadcasted_iota(jnp.int32, st.shape, 1)
        qpos = i * bq + jnp.where(col >= bq, col - bq, col)
        st = jnp.where(key <= qpos, st, NEG_INF)
        s_ref[slot][...] = st
        cmax_ref[slot][...] = jnp.max(st, axis=0, keepdims=True)

    def body(j, carry):
        @pl.when(j % 2 == 0)
        def _():
            stage(j, 0, j + 1)

        @pl.when(j % 2 == 1)
        def _():
            stage(j, 1, j + 1)

        return carry

    def drain(par):
        mask_diagonal(par)
        stage(n_blocks - 1, par, None)

    first_scores()
    lax.fori_loop(0, n_blocks - 1, body, 0)

    @pl.when(n_blocks % 2 == 1)
    def _():
        drain(0)

    @pl.when(n_blocks % 2 == 0)
    def _():
        drain(1)

    lam = (jnp.exp(jnp.sum(lq1_ref[...] * lk1_ref[...], axis=-1, keepdims=True))
           - jnp.exp(jnp.sum(lq2_ref[...] * lk2_ref[...], axis=-1, keepdims=True)) + LAM_INIT)
    ot = acc_ref[...] / l_ref[...]
    dt = ot[:, 0:bq] - lam * ot[:, bq:2 * bq]
    yt = dt * lax.rsqrt(jnp.mean(dt * dt, axis=0, keepdims=True) + NORM_EPS)
    o_ref[...] = (yt.T * g_ref[...] * (1.0 - LAM_INIT)).astype(BF16)


def _diff_attention(proj, qt, vt, lq1, lk1, lq2, lk2, g):
    s_len = proj.shape[1]
    assert DIFF_BK % DIFF_BQ == 0 and s_len % DIFF_BK == 0
    vec = lambda a: a.reshape(1, -1).astype(F32)
    small = lambda width: pl.BlockSpec((1, width), lambda h, i: (0, 0))
    return pl.pallas_call(
        _diff_kernel,
        grid=(N_HEADS, s_len // DIFF_BQ),
        in_specs=[
            small(QK_DIM), small(QK_DIM), small(QK_DIM), small(QK_DIM),
            pl.BlockSpec((1, 1, HEAD, DIFF_BQ), lambda h, i: (h, i, 0, 0)),
            pl.BlockSpec((1, s_len, HEAD), lambda h, i: (N_HEADS + h, 0, 0)),
            pl.BlockSpec((1, s_len // CHUNK, HEAD, CHUNK), lambda h, i: (h, 0, 0, 0)),
            small(HEAD),
        ],
        out_specs=pl.BlockSpec((DIFF_BQ, HEAD), lambda h, i: (i, h)),
        out_shape=jax.ShapeDtypeStruct((s_len, N_HEADS * HEAD), BF16),
        scratch_shapes=[
            pltpu.VMEM((HEAD, 2 * DIFF_BQ), BF16),
            pltpu.VMEM((DIFF_BK, 2 * DIFF_BQ), F32),
            pltpu.VMEM((DIFF_BK, 2 * DIFF_BQ), F32),
            pltpu.VMEM((1, 2 * DIFF_BQ), F32),
            pltpu.VMEM((1, 2 * DIFF_BQ), F32),
            pltpu.VMEM((DIFF_BK, 2 * DIFF_BQ), BF16),
            pltpu.VMEM((1, 2 * DIFF_BQ), F32),
            pltpu.VMEM((1, 2 * DIFF_BQ), F32),
            pltpu.VMEM((HEAD, 2 * DIFF_BQ), F32),
        ],
        compiler_params=_cparams(2),
        name="diff_attn",
    )(vec(lq1), vec(lk1), vec(lq2), vec(lk2), qt, proj, vt, vec(g))


def _sb_kernel(q_ref, k_ref, v_ref, g_ref, o_ref, acc_ref, stick_ref, *, blk, group):
    scale = HEAD ** -0.5
    r = lax.broadcasted_iota(jnp.int32, (blk, blk), 0)
    c = lax.broadcasted_iota(jnp.int32, (blk, blk), 1)
    later = jnp.where(r > c, 1.0, 0.0).astype(BF16)
    strict = c < r

    def block(q, j, stick, masked):
        start = pl.multiple_of(j * blk, blk)
        kj = k_ref[0, pl.ds(start, blk), :]
        vj = v_ref[0, pl.ds(start, blk), :]
        z = lax.dot_general(q, kj, (((1,), (1,)), ((), ())), preferred_element_type=F32) * scale
        log_beta = jnp.minimum(z, 0.0) - jnp.log(1.0 + jnp.exp(-jnp.abs(z)))
        log_rest = log_beta - z
        if masked:
            log_rest = jnp.where(strict, log_rest, 0.0)
        hi = log_rest.astype(BF16)
        lo = (log_rest - hi.astype(F32)).astype(BF16)
        within = (jnp.dot(hi, later, preferred_element_type=F32)
                  + jnp.dot(lo, later, preferred_element_type=F32))
        a = jnp.exp(log_beta + within + stick)
        if masked:
            a = jnp.where(strict, a, 0.0)
        return (jnp.dot(a.astype(BF16), vj, preferred_element_type=F32),
                jnp.sum(log_rest, axis=1, keepdims=True))

    first = pl.program_id(1) * group
    for g in range(group):
        i = first + g
        q = q_ref[0, g * blk:(g + 1) * blk, :]
        pv_diag, spent_diag = block(q, i, 0.0, True)
        pv_prev, spent_prev = block(q, jnp.maximum(i - 1, 0), spent_diag, False)
        has_prev = i > 0
        acc_ref[g] = pv_diag + jnp.where(has_prev, pv_prev, 0.0)
        stick_ref[g] = jnp.broadcast_to(spent_diag + jnp.where(has_prev, spent_prev, 0.0), stick_ref.shape[1:])

    for g in range(group):
        q = q_ref[0, g * blk:(g + 1) * blk, :]

        def alive():
            return jnp.max(stick_ref[g]) > SB_DEAD_LOG

        def cond(state):
            j, go = state
            return jnp.logical_and(j >= 0, go)

        def body(state):
            j, _ = state
            pv, spent = block(q, j, stick_ref[g, :, 0:1], False)
            acc_ref[g] += pv
            stick_ref[g] += spent
            return j - 1, alive()

        lax.while_loop(cond, body, (first + g - 2, alive()))
        o_ref[g * blk:(g + 1) * blk, :] = (_rms(acc_ref[g]) * g_ref[...]).astype(BF16)


def _sb_attention(proj, g, blk=256, group=4):
    s_len = proj.shape[1]
    rows = blk * group
    return pl.pallas_call(
        functools.partial(_sb_kernel, blk=blk, group=group),
        grid=(N_HEADS, s_len // rows),
        in_specs=[
            pl.BlockSpec((1, rows, HEAD), lambda h, i: (3 * N_HEADS + h, i, 0)),
            pl.BlockSpec((1, s_len, HEAD), lambda h, i: (4 * N_HEADS + h, 0, 0)),
            pl.BlockSpec((1, s_len, HEAD), lambda h, i: (5 * N_HEADS + h, 0, 0)),
            pl.BlockSpec((1, HEAD), lambda h, i: (0, 0)),
        ],
        out_specs=pl.BlockSpec((rows, HEAD), lambda h, i: (i, h)),
        out_shape=jax.ShapeDtypeStruct((s_len, N_HEADS * HEAD), BF16),
        scratch_shapes=[
            pltpu.VMEM((group, blk, HEAD), F32),
            pltpu.VMEM((group, blk, HEAD), F32),
        ],
        compiler_params=_cparams(2),
        name="sb_attn",
    )(proj, proj, proj, g.reshape(1, HEAD).astype(F32))


def _outproj_kernel(md_ref, ms_ref, x_ref, w_ref, g_ref, x1_ref, h2_ref):
    half = md_ref.shape[1]
    y = (jnp.dot(md_ref[...], w_ref[0:half, :], preferred_element_type=F32)
         + jnp.dot(ms_ref[...], w_ref[half:2 * half, :], preferred_element_type=F32))
    x1 = x_ref[...] + y
    x1_ref[...] = x1
    h2_ref[...] = (_rms(x1) * g_ref[...]).astype(BF16)


def _out_project(mixed_diff, mixed_sb, x2, w_out_bf16, ln2, tm=512):
    s_len = x2.shape[0]
    half = N_HEADS * HEAD
    return pl.pallas_call(
        _outproj_kernel,
        grid=(s_len // tm,),
        in_specs=[
            pl.BlockSpec((tm, half), lambda m: (m, 0)),
            pl.BlockSpec((tm, half), lambda m: (m, 0)),
            pl.BlockSpec((tm, D_MODEL), lambda m: (m, 0)),
            pl.BlockSpec((2 * half, D_MODEL), lambda m: (0, 0)),
            pl.BlockSpec((1, D_MODEL), lambda m: (0, 0)),
        ],
        out_specs=[
            pl.BlockSpec((tm, D_MODEL), lambda m: (m, 0)),
            pl.BlockSpec((tm, D_MODEL), lambda m: (m, 0)),
        ],
        out_shape=[
            jax.ShapeDtypeStruct((s_len, D_MODEL), F32),
            jax.ShapeDtypeStruct((s_len, D_MODEL), BF16),
        ],
        compiler_params=_cparams(1),
        name="proj_out",
    )(mixed_diff, mixed_sb, x2, w_out_bf16, ln2.reshape(1, D_MODEL))


def _mlp_kernel(h_ref, w1_ref, w2_ref, x1_ref, g_ref, o_ref, acc_ref):
    f = pl.program_id(1)

    @pl.when(f == 0)
    def _():
        acc_ref[...] = x1_ref[...]

    a = jnp.dot(h_ref[...], w1_ref[...], preferred_element_type=F32)
    a = jnp.square(jnp.maximum(a, 0.0)).astype(BF16)
    acc_ref[...] += jnp.dot(a, w2_ref[...], preferred_element_type=F32)

    @pl.when(f == pl.num_programs(1) - 1)
    def _():
        o_ref[...] = _rms(acc_ref[...]) * g_ref[...]


def _mlp(h2, x1, w1_bf16, w2_bf16, ln_f, tm=512, tf=1024):
    s_len = h2.shape[0]
    return pl.pallas_call(
        _mlp_kernel,
        grid=(s_len // tm, D_FF // tf),
        in_specs=[
            pl.BlockSpec((tm, D_MODEL), lambda m, f: (m, 0)),
            pl.BlockSpec((D_MODEL, tf), lambda m, f: (0, f)),
            pl.BlockSpec((tf, D_MODEL), lambda m, f: (f, 0)),
            pl.BlockSpec((tm, D_MODEL), lambda m, f: (m, 0)),
            pl.BlockSpec((1, D_MODEL), lambda m, f: (0, 0)),
        ],
        out_specs=pl.BlockSpec((tm, D_MODEL), lambda m, f: (m, 0)),
        out_shape=jax.ShapeDtypeStruct((s_len, D_MODEL), F32),
        scratch_shapes=[pltpu.VMEM((tm, D_MODEL), F32)],
        compiler_params=_cparams(2),
        name="mlp",
    )(h2, w1_bf16, w2_bf16, x1, ln_f.reshape(1, D_MODEL))


def kernel(x, ln1, w_in, lambda_q1, lambda_k1, lambda_q2, lambda_k2, diff_head_norm, sb_head_norm,
           w_out, ln2, w_mlp_in, w_mlp_out, ln_f):
    b, s_len, _ = x.shape
    assert b == 1 and ln1.shape[0] == 1
    x2 = x.reshape(s_len, D_MODEL)
    proj, qt, vt = _project(x2, ln1[0], w_in[0].astype(BF16))
    mixed_diff = _diff_attention(proj, qt, vt, lambda_q1[0], lambda_k1[0], lambda_q2[0], lambda_k2[0], diff_head_norm[0])
    mixed_sb = _sb_attention(proj, sb_head_norm[0])
    x1, h2 = _out_project(mixed_diff, mixed_sb, x2, w_out[0].astype(BF16), ln2[0])
    out = _mlp(h2, x1, w_mlp_in[0].astype(BF16), w_mlp_out[0].astype(BF16), ln_f)
    return out.reshape(b, s_len, D_MODEL)
```

```python
import functools
import math

import jax
import jax.numpy as jnp
import numpy as np
from jax import lax
from jax.experimental import pallas as pl
from jax.experimental.pallas import tpu as pltpu

F32 = jnp.float32
BF16 = jnp.bfloat16

D_MODEL = 2048
HEAD = 128
N_HEADS = 8
QK_DIM = 64
D_FF = 4 * D_MODEL
PROJ_WIDTH = 6 * N_HEADS * HEAD
ROT_COLS = 2 * N_HEADS * HEAD
ROT_DIMS = QK_DIM // 4
ROPE_THETA = 500000.0
NORM_EPS = 1e-6
NEG_INF = -1e30
LAM_INIT = 0.8 - 0.6 * math.exp(-0.3 * 0)
SB_DEAD_LOG = -105.0
DIFF_BQ = 512
DIFF_BK = 1024
ONES_ROWS = 16
CHUNK = 256
LOG2E = math.log2(math.e)
DIFF_QSCALE = QK_DIM ** -0.5 * LOG2E

VMEM_LIMIT = 56 * 1024 * 1024


def _cparams(n_axes, flags=None):
    return pltpu.CompilerParams(dimension_semantics=("arbitrary",) * n_axes,
                                vmem_limit_bytes=VMEM_LIMIT, flags=flags)


def _rms(x):
    return x * lax.rsqrt(jnp.mean(x * x, axis=-1, keepdims=True) + NORM_EPS)


def _proj_kernel(x_ref, g_ref, w_ref, cos_ref, sup_ref, sdn_ref, o_ref, qt_ref, vt_ref, h_ref, y_ref, *, tn):
    n = pl.program_id(1)
    n_heads_tile = tn // HEAD
    assert n_heads_tile == N_HEADS

    @pl.when(n == 0)
    def _():
        h_ref[...] = (_rms(x_ref[...]) * g_ref[...]).astype(BF16)

    def project(with_rotary):
        acc = jnp.dot(h_ref[...], w_ref[...], preferred_element_type=F32)
        for c in range(n_heads_tile):
            y = acc[:, c * HEAD:(c + 1) * HEAD]
            if with_rotary:
                y = (y * cos_ref[...] + pltpu.roll(y, ROT_DIMS // 2, 1) * sup_ref[...]
                     + pltpu.roll(y, HEAD - ROT_DIMS // 2, 1) * sdn_ref[...])
            o_ref[c] = y.astype(BF16)
            y_ref[:, c * HEAD:(c + 1) * HEAD] = y

    @pl.when(n < ROT_COLS // tn)
    def _():
        project(True)

    @pl.when(n >= ROT_COLS // tn)
    def _():
        project(False)

    def store_transposed(dst_ref, c, scale):
        blocks, _, width = dst_ref.shape[1:]
        for b in range(blocks):
            y = y_ref[b * width:(b + 1) * width, c * HEAD:(c + 1) * HEAD]
            dst_ref[c, b] = (y * scale).T.astype(BF16)

    @pl.when(n == 0)
    def _():
        for c in range(n_heads_tile):
            store_transposed(qt_ref, c, DIFF_QSCALE)

    @pl.when(n == 2)
    def _():
        for c in range(n_heads_tile):
            store_transposed(vt_ref, c, 1.0)


def _rotary_tables(s_len):
    half = ROT_DIMS // 2
    inv_freq = ROPE_THETA ** (-jnp.arange(0, ROT_DIMS, 2, dtype=F32) / ROT_DIMS)
    comp = np.arange(HEAD) % QK_DIM
    first, second = comp < half, (comp >= half) & (comp < ROT_DIMS)
    ang = jnp.arange(s_len, dtype=F32)[:, None] * inv_freq[comp % half][None, :]
    cos, sin = jnp.cos(ang), jnp.sin(ang)
    cos_t = jnp.where(first | second, cos, 1.0)
    sup_t = jnp.where(second, sin, 0.0)
    sdn_t = jnp.where(first, -sin, 0.0)
    return cos_t, sup_t, sdn_t


def _project(x2, ln1, w_in_bf16, tm=512, tn=N_HEADS * HEAD):
    s_len = x2.shape[0]
    cos, sup, sdn = _rotary_tables(s_len)
    kern = functools.partial(_proj_kernel, tn=tn)
    return pl.pallas_call(
        kern,
        grid=(s_len // tm, PROJ_WIDTH // tn),
        in_specs=[
            pl.BlockSpec((tm, D_MODEL), lambda m, n: (m, 0)),
            pl.BlockSpec((1, D_MODEL), lambda m, n: (0, 0)),
            pl.BlockSpec((D_MODEL, tn), lambda m, n: (0, n)),
            pl.BlockSpec((tm, HEAD), lambda m, n: (m, 0)),
            pl.BlockSpec((tm, HEAD), lambda m, n: (m, 0)),
            pl.BlockSpec((tm, HEAD), lambda m, n: (m, 0)),
        ],
        out_specs=[
            pl.BlockSpec((tn // HEAD, tm, HEAD), lambda m, n: (n, m, 0)),
            pl.BlockSpec((N_HEADS, tm // DIFF_BQ, HEAD, DIFF_BQ), lambda m, n: (0, m, 0, 0)),
            pl.BlockSpec((N_HEADS, tm // CHUNK, HEAD, CHUNK), lambda m, n: (0, m, 0, 0)),
        ],
        out_shape=[
            jax.ShapeDtypeStruct((PROJ_WIDTH // HEAD, s_len, HEAD), BF16),
            jax.ShapeDtypeStruct((N_HEADS, s_len // DIFF_BQ, HEAD, DIFF_BQ), BF16),
            jax.ShapeDtypeStruct((N_HEADS, s_len // CHUNK, HEAD, CHUNK), BF16),
        ],
        scratch_shapes=[
            pltpu.VMEM((tm, D_MODEL), BF16),
            pltpu.VMEM((tm, tn), F32),
        ],
        compiler_params=_cparams(2),
        name="proj_in",
    )(x2, ln1.reshape(1, D_MODEL), w_in_bf16, cos, sup, sdn)


def _diff_kernel(lq1_ref, lk1_ref, lq2_ref, lk2_ref, qt_ref, k_ref, vt_ref, g_ref, o_ref,
                 qqt_ref, s0_ref, s1_ref, cmax0_ref, cmax1_ref, p_ref, m_ref, acc_ref):
    bq, bk = DIFF_BQ, DIFF_BK
    n_chunks = bk // CHUNK
    i = pl.program_id(1)
    qt = qt_ref[0, 0]
    sub = lax.broadcasted_iota(jnp.int32, (HEAD, bq), 0)
    zero = jnp.zeros_like(qt)
    qqt_ref[:, 0:bq] = jnp.where(sub < QK_DIM, qt, zero)
    qqt_ref[:, bq:2 * bq] = jnp.where(sub >= QK_DIM, qt, zero)
    m_ref[...] = jnp.full(m_ref.shape, NEG_INF, F32)
    acc_ref[...] = jnp.zeros(acc_ref.shape, F32)
    s_ref, cmax_ref = (s0_ref, s1_ref), (cmax0_ref, cmax1_ref)
    last = (i * bq) // bk
    ones = jnp.ones((ONES_ROWS, CHUNK), BF16)

    def scores_chunk(j, slot, r):
        rows = pl.ds(pl.multiple_of(j * bk + r * CHUNK, CHUNK), CHUNK)
        st = jnp.dot(k_ref[0, rows, :], qqt_ref[...], preferred_element_type=F32)
        s_ref[slot][r * CHUNK:(r + 1) * CHUNK, :] = st
        cmax_ref[slot][r:r + 1, :] = jnp.max(st, axis=0, keepdims=True)

    def softmax_chunk(slot, r, m_next):
        pt = jnp.exp2(s_ref[slot][r * CHUNK:(r + 1) * CHUNK, :] - m_next)
        p_ref[r * CHUNK:(r + 1) * CHUNK, :] = pt.astype(BF16)

    def values_chunk(j, r, pv):
        lhs = jnp.concatenate([vt_ref[0, j * n_chunks + r], ones], axis=0)
        d = jnp.dot(lhs, p_ref[r * CHUNK:(r + 1) * CHUNK, :], preferred_element_type=F32)
        return d if pv is None else pv + d

    def stage(j, par, next_block, used_chunks=n_chunks):
        m_prev = m_ref[...]
        m_next = jnp.maximum(m_prev, jnp.max(cmax_ref[par][0:used_chunks, :], axis=0, keepdims=True))
        alpha = jnp.exp2(m_prev - m_next)
        pv = None
        for r in range(n_chunks):
            if next_block is not None:
                scores_chunk(next_block, 1 - par, r)
            if r < used_chunks:
                softmax_chunk(par, r, m_next)
                pv = values_chunk(j, r, pv)
        m_ref[...] = m_next
        acc_ref[...] = alpha * acc_ref[...] + pv

    def body(j, carry):
        @pl.when(j % 2 == 0)
        def _():
            stage(j, 0, j + 1)

        @pl.when(j % 2 == 1)
        def _():
            stage(j, 1, j + 1)

        return carry

    def drain(par, first_masked):
        used = first_masked + bq // CHUNK
        for r in range(first_masked, used):
            st = s_ref[par][r * CHUNK:(r + 1) * CHUNK, :]
            key = r * CHUNK - first_masked * CHUNK + lax.broadcasted_iota(jnp.int32, st.shape, 0)
            col = lax.broadcasted_iota(jnp.int32, st.shape, 1)
            st = jnp.where(key <= jnp.where(col >= bq, col - bq, col), st, NEG_INF)
            s_ref[par][r * CHUNK:(r + 1) * CHUNK, :] = st
            cmax_ref[par][r:r + 1, :] = jnp.max(st, axis=0, keepdims=True)
        stage(last, par, None, used)

    for r in range(n_chunks):
        scores_chunk(0, 0, r)
    lax.fori_loop(0, last, body, 0)

    q_per_k = bk // bq
    for par in range(2):
        for pos in range(q_per_k):
            @pl.when(jnp.logical_and(last % 2 == par, i % q_per_k == pos))
            def _(par=par, pos=pos):
                drain(par, pos * (bq // CHUNK))

    lam = (jnp.exp(jnp.sum(lq1_ref[...] * lk1_ref[...], axis=-1, keepdims=True))
           - jnp.exp(jnp.sum(lq2_ref[...] * lk2_ref[...], axis=-1, keepdims=True)) + LAM_INIT)
    ot = acc_ref[0:HEAD, :] / acc_ref[HEAD:HEAD + 1, :]
    dt = ot[:, 0:bq] - lam * ot[:, bq:2 * bq]
    yt = dt * lax.rsqrt(jnp.mean(dt * dt, axis=0, keepdims=True) + NORM_EPS)
    o_ref[...] = (yt.T * g_ref[...] * (1.0 - LAM_INIT)).astype(BF16)


def _diff_attention(proj, qt, vt, lq1, lk1, lq2, lk2, g):
    s_len = proj.shape[1]
    assert DIFF_BK % DIFF_BQ == 0 and s_len % DIFF_BK == 0
    vec = lambda a: a.reshape(1, -1).astype(F32)
    small = lambda width: pl.BlockSpec((1, width), lambda h, i: (0, 0))
    return pl.pallas_call(
        _diff_kernel,
        grid=(N_HEADS, s_len // DIFF_BQ),
        in_specs=[
            small(QK_DIM), small(QK_DIM), small(QK_DIM), small(QK_DIM),
            pl.BlockSpec((1, 1, HEAD, DIFF_BQ), lambda h, i: (h, i, 0, 0)),
            pl.BlockSpec((1, s_len, HEAD), lambda h, i: (N_HEADS + h, 0, 0)),
            pl.BlockSpec((1, s_len // CHUNK, HEAD, CHUNK), lambda h, i: (h, 0, 0, 0)),
            small(HEAD),
        ],
        out_specs=pl.BlockSpec((DIFF_BQ, HEAD), lambda h, i: (i, h)),
        out_shape=jax.ShapeDtypeStruct((s_len, N_HEADS * HEAD), BF16),
        scratch_shapes=[
            pltpu.VMEM((HEAD, 2 * DIFF_BQ), BF16),
            pltpu.VMEM((DIFF_BK, 2 * DIFF_BQ), F32),
            pltpu.VMEM((DIFF_BK, 2 * DIFF_BQ), F32),
            pltpu.VMEM((8, 2 * DIFF_BQ), F32),
            pltpu.VMEM((8, 2 * DIFF_BQ), F32),
            pltpu.VMEM((DIFF_BK, 2 * DIFF_BQ), BF16),
            pltpu.VMEM((1, 2 * DIFF_BQ), F32),
            pltpu.VMEM((HEAD + ONES_ROWS, 2 * DIFF_BQ), F32),
        ],
        compiler_params=_cparams(2),
        name="diff_attn",
    )(vec(lq1), vec(lk1), vec(lq2), vec(lk2), qt, proj, vt, vec(g))


def _sb_kernel(q_ref, k_ref, v_ref, g_ref, o_ref, acc_ref, stick_ref, *, blk, group):
    scale = HEAD ** -0.5
    r = lax.broadcasted_iota(jnp.int32, (blk, blk), 0)
    c = lax.broadcasted_iota(jnp.int32, (blk, blk), 1)
    later = jnp.where(r > c, 1.0, 0.0).astype(BF16)
    strict = c < r

    def block(q, j, stick, masked):
        start = pl.multiple_of(j * blk, blk)
        kj = k_ref[0, pl.ds(start, blk), :]
        vj = v_ref[0, pl.ds(start, blk), :]
        z = lax.dot_general(q, kj, (((1,), (1,)), ((), ())), preferred_element_type=F32) * scale
        log_beta = jnp.minimum(z, 0.0) - jnp.log(1.0 + jnp.exp(-jnp.abs(z)))
        log_rest = log_beta - z
        if masked:
            log_rest = jnp.where(strict, log_rest, 0.0)
        hi = log_rest.astype(BF16)
        lo = (log_rest - hi.astype(F32)).astype(BF16)
        within = (jnp.dot(hi, later, preferred_element_type=F32)
                  + jnp.dot(lo, later, preferred_element_type=F32))
        a = jnp.exp(log_beta + within + stick)
        if masked:
            a = jnp.where(strict, a, 0.0)
        return (jnp.dot(a.astype(BF16), vj, preferred_element_type=F32),
                jnp.sum(log_rest, axis=1, keepdims=True))

    first = pl.program_id(1) * group
    for g in range(group):
        i = first + g
        q = q_ref[0, g * blk:(g + 1) * blk, :]
        pv_diag, spent_diag = block(q, i, 0.0, True)
        pv_prev, spent_prev = block(q, jnp.maximum(i - 1, 0), spent_diag, False)
        has_prev = i > 0
        acc_ref[g] = pv_diag + jnp.where(has_prev, pv_prev, 0.0)
        stick_ref[g] = jnp.broadcast_to(spent_diag + jnp.where(has_prev, spent_prev, 0.0), stick_ref.shape[1:])

    for g in range(group):
        q = q_ref[0, g * blk:(g + 1) * blk, :]

        def alive():
            return jnp.max(stick_ref[g]) > SB_DEAD_LOG

        def cond(state):
            j, go = state
            return jnp.logical_and(j >= 0, go)

        def body(state):
            j, _ = state
            pv, spent = block(q, j, stick_ref[g, :, 0:1], False)
            acc_ref[g] += pv
            stick_ref[g] += spent
            return j - 1, alive()

        lax.while_loop(cond, body, (first + g - 2, alive()))
        o_ref[g * blk:(g + 1) * blk, :] = (_rms(acc_ref[g]) * g_ref[...]).astype(BF16)


def _sb_attention(proj, g, blk=256, group=4):
    s_len = proj.shape[1]
    rows = blk * group
    return pl.pallas_call(
        functools.partial(_sb_kernel, blk=blk, group=group),
        grid=(N_HEADS, s_len // rows),
        in_specs=[
            pl.BlockSpec((1, rows, HEAD), lambda h, i: (3 * N_HEADS + h, i, 0)),
            pl.BlockSpec((1, s_len, HEAD), lambda h, i: (4 * N_HEADS + h, 0, 0)),
            pl.BlockSpec((1, s_len, HEAD), lambda h, i: (5 * N_HEADS + h, 0, 0)),
            pl.BlockSpec((1, HEAD), lambda h, i: (0, 0)),
        ],
        out_specs=pl.BlockSpec((rows, HEAD), lambda h, i: (i, h)),
        out_shape=jax.ShapeDtypeStruct((s_len, N_HEADS * HEAD), BF16),
        scratch_shapes=[
            pltpu.VMEM((group, blk, HEAD), F32),
            pltpu.VMEM((group, blk, HEAD), F32),
        ],
        compiler_params=_cparams(2),
        name="sb_attn",
    )(proj, proj, proj, g.reshape(1, HEAD).astype(F32))


def _outproj_kernel(md_ref, ms_ref, x_ref, w_ref, g_ref, x1_ref, h2_ref):
    half = md_ref.shape[1]
    y = (jnp.dot(md_ref[...], w_ref[0:half, :], preferred_element_type=F32)
         + jnp.dot(ms_ref[...], w_ref[half:2 * half, :], preferred_element_type=F32))
    x1 = x_ref[...] + y
    x1_ref[...] = x1
    h2_ref[...] = (_rms(x1) * g_ref[...]).astype(BF16)


def _out_project(mixed_diff, mixed_sb, x2, w_out_bf16, ln2, tm=512):
    s_len = x2.shape[0]
    half = N_HEADS * HEAD
    return pl.pallas_call(
        _outproj_kernel,
        grid=(s_len // tm,),
        in_specs=[
            pl.BlockSpec((tm, half), lambda m: (m, 0)),
            pl.BlockSpec((tm, half), lambda m: (m, 0)),
            pl.BlockSpec((tm, D_MODEL), lambda m: (m, 0)),
            pl.BlockSpec((2 * half, D_MODEL), lambda m: (0, 0)),
            pl.BlockSpec((1, D_MODEL), lambda m: (0, 0)),
        ],
        out_specs=[
            pl.BlockSpec((tm, D_MODEL), lambda m: (m, 0)),
            pl.BlockSpec((tm, D_MODEL), lambda m: (m, 0)),
        ],
        out_shape=[
            jax.ShapeDtypeStruct((s_len, D_MODEL), F32),
            jax.ShapeDtypeStruct((s_len, D_MODEL), BF16),
        ],
        compiler_params=_cparams(1),
        name="proj_out",
    )(mixed_diff, mixed_sb, x2, w_out_bf16, ln2.reshape(1, D_MODEL))


def _mlp_kernel(h_ref, w1_ref, w2_ref, x1_ref, g_ref, o_ref, acc_ref):
    f = pl.program_id(1)

    @pl.when(f == 0)
    def _():
        acc_ref[...] = x1_ref[...]

    a = jnp.dot(h_ref[...], w1_ref[...], preferred_element_type=F32)
    a = jnp.square(jnp.maximum(a, 0.0)).astype(BF16)
    acc_ref[...] += jnp.dot(a, w2_ref[...], preferred_element_type=F32)

    @pl.when(f == pl.num_programs(1) - 1)
    def _():
        o_ref[...] = _rms(acc_ref[...]) * g_ref[...]


def _mlp(h2, x1, w1_bf16, w2_bf16, ln_f, tm=512, tf=1024):
    s_len = h2.shape[0]
    return pl.pallas_call(
        _mlp_kernel,
        grid=(s_len // tm, D_FF // tf),
        in_specs=[
            pl.BlockSpec((tm, D_MODEL), lambda m, f: (m, 0)),
            pl.BlockSpec((D_MODEL, tf), lambda m, f: (0, f)),
            pl.BlockSpec((tf, D_MODEL), lambda m, f: (f, 0)),
            pl.BlockSpec((tm, D_MODEL), lambda m, f: (m, 0)),
            pl.BlockSpec((1, D_MODEL), lambda m, f: (0, 0)),
        ],
        out_specs=pl.BlockSpec((tm, D_MODEL), lambda m, f: (m, 0)),
        out_shape=jax.ShapeDtypeStruct((s_len, D_MODEL), F32),
        scratch_shapes=[pltpu.VMEM((tm, D_MODEL), F32)],
        compiler_params=_cparams(2),
        name="mlp",
    )(h2, w1_bf16, w2_bf16, x1, ln_f.reshape(1, D_MODEL))


def kernel(x, ln1, w_in, lambda_q1, lambda_k1, lambda_q2, lambda_k2, diff_head_norm, sb_head_norm,
           w_out, ln2, w_mlp_in, w_mlp_out, ln_f):
    b, s_len, _ = x.shape
    assert b == 1 and ln1.shape[0] == 1
    x2 = x.reshape(s_len, D_MODEL)
    proj, qt, vt = _project(x2, ln1[0], w_in[0].astype(BF16))
    mixed_diff = _diff_attention(proj, qt, vt, lambda_q1[0], lambda_k1[0], lambda_q2[0], lambda_k2[0], diff_head_norm[0])
    mixed_sb = _sb_attention(proj, sb_head_norm[0])
    x1, h2 = _out_project(mixed_diff, mixed_sb, x2, w_out[0].astype(BF16), ln2[0])
    out = _mlp(h2, x1, w_mlp_in[0].astype(BF16), w_mlp_out[0].astype(BF16), ln_f)
    return out.reshape(b, s_len, D_MODEL)
```

```python
import functools
import math

import jax
import jax.numpy as jnp
import numpy as np
from jax import lax
from jax.experimental import pallas as pl
from jax.experimental.pallas import tpu as pltpu

F32 = jnp.float32
BF16 = jnp.bfloat16
F8 = jnp.float8_e4m3fn
F8_MAX = 448.0

D_MODEL = 2048
HEAD = 128
N_HEADS = 8
QK_DIM = 64
D_FF = 4 * D_MODEL
PROJ_WIDTH = 6 * N_HEADS * HEAD
ROT_COLS = 2 * N_HEADS * HEAD
ROT_DIMS = QK_DIM // 4
ROPE_THETA = 500000.0
NORM_EPS = 1e-6
NEG_INF = -1e30
LAM_INIT = 0.8 - 0.6 * math.exp(-0.3 * 0)
SB_DEAD_LOG = -105.0
DIFF_BQ = 512
DIFF_BK = 1024
ONES_ROWS = 16
CHUNK = 256
LOG2E = math.log2(math.e)
DIFF_QSCALE = QK_DIM ** -0.5 * LOG2E

VMEM_LIMIT = 56 * 1024 * 1024


def _cparams(n_axes, flags=None):
    return pltpu.CompilerParams(dimension_semantics=("arbitrary",) * n_axes,
                                vmem_limit_bytes=VMEM_LIMIT, flags=flags)


def _rms(x):
    return x * lax.rsqrt(jnp.mean(x * x, axis=-1, keepdims=True) + NORM_EPS)


def _proj_kernel(x_ref, g_ref, w_ref, cos_ref, sup_ref, sdn_ref, o_ref, qt_ref, vt_ref, h_ref, y_ref, *, tn):
    n = pl.program_id(1)
    n_heads_tile = tn // HEAD
    assert n_heads_tile == N_HEADS

    @pl.when(n == 0)
    def _():
        h_ref[...] = (_rms(x_ref[...]) * g_ref[...]).astype(BF16)

    def project(with_rotary):
        acc = jnp.dot(h_ref[...], w_ref[...], preferred_element_type=F32)
        for c in range(n_heads_tile):
            y = acc[:, c * HEAD:(c + 1) * HEAD]
            if with_rotary:
                y = (y * cos_ref[...] + pltpu.roll(y, ROT_DIMS // 2, 1) * sup_ref[...]
                     + pltpu.roll(y, HEAD - ROT_DIMS // 2, 1) * sdn_ref[...])
            o_ref[c] = y.astype(BF16)
            y_ref[:, c * HEAD:(c + 1) * HEAD] = y

    @pl.when(n < ROT_COLS // tn)
    def _():
        project(True)

    @pl.when(n >= ROT_COLS // tn)
    def _():
        project(False)

    def store_transposed(dst_ref, c, scale):
        blocks, _, width = dst_ref.shape[1:]
        for b in range(blocks):
            y = y_ref[b * width:(b + 1) * width, c * HEAD:(c + 1) * HEAD]
            dst_ref[c, b] = (y * scale).T.astype(BF16)

    @pl.when(n == 0)
    def _():
        for c in range(n_heads_tile):
            store_transposed(qt_ref, c, DIFF_QSCALE)

    @pl.when(n == 2)
    def _():
        for c in range(n_heads_tile):
            store_transposed(vt_ref, c, 1.0)


def _rotary_tables(s_len):
    half = ROT_DIMS // 2
    inv_freq = ROPE_THETA ** (-jnp.arange(0, ROT_DIMS, 2, dtype=F32) / ROT_DIMS)
    comp = np.arange(HEAD) % QK_DIM
    first, second = comp < half, (comp >= half) & (comp < ROT_DIMS)
    ang = jnp.arange(s_len, dtype=F32)[:, None] * inv_freq[comp % half][None, :]
    cos, sin = jnp.cos(ang), jnp.sin(ang)
    cos_t = jnp.where(first | second, cos, 1.0)
    sup_t = jnp.where(second, sin, 0.0)
    sdn_t = jnp.where(first, -sin, 0.0)
    return cos_t, sup_t, sdn_t


def _project(x2, ln1, w_in_bf16, tm=1024, tn=N_HEADS * HEAD):
    s_len = x2.shape[0]
    cos, sup, sdn = _rotary_tables(s_len)
    kern = functools.partial(_proj_kernel, tn=tn)
    return pl.pallas_call(
        kern,
        grid=(s_len // tm, PROJ_WIDTH // tn),
        in_specs=[
            pl.BlockSpec((tm, D_MODEL), lambda m, n: (m, 0)),
            pl.BlockSpec((1, D_MODEL), lambda m, n: (0, 0)),
            pl.BlockSpec((D_MODEL, tn), lambda m, n: (0, n)),
            pl.BlockSpec((tm, HEAD), lambda m, n: (m, 0)),
            pl.BlockSpec((tm, HEAD), lambda m, n: (m, 0)),
            pl.BlockSpec((tm, HEAD), lambda m, n: (m, 0)),
        ],
        out_specs=[
            pl.BlockSpec((tn // HEAD, tm, HEAD), lambda m, n: (n, m, 0)),
            pl.BlockSpec((N_HEADS, tm // DIFF_BQ, HEAD, DIFF_BQ), lambda m, n: (0, m, 0, 0)),
            pl.BlockSpec((N_HEADS, tm // CHUNK, HEAD, CHUNK), lambda m, n: (0, m, 0, 0)),
        ],
        out_shape=[
            jax.ShapeDtypeStruct((PROJ_WIDTH // HEAD, s_len, HEAD), BF16),
            jax.ShapeDtypeStruct((N_HEADS, s_len // DIFF_BQ, HEAD, DIFF_BQ), BF16),
            jax.ShapeDtypeStruct((N_HEADS, s_len // CHUNK, HEAD, CHUNK), BF16),
        ],
        scratch_shapes=[
            pltpu.VMEM((tm, D_MODEL), BF16),
            pltpu.VMEM((tm, tn), F32),
        ],
        compiler_params=_cparams(2),
        name="proj_in",
    )(x2, ln1.reshape(1, D_MODEL), w_in_bf16, cos, sup, sdn)


def _diff_kernel(scale_ref, lq1_ref, lk1_ref, lq2_ref, lk2_ref, qt_ref, k_ref, vt_ref, g_ref, o_ref,
                 rq_ref, ka_ref, s0_ref, s1_ref, cmax0_ref, cmax1_ref, p_ref, m_ref, acc_ref):
    bq, bk = DIFF_BQ, DIFF_BK
    n_chunks = bk // CHUNK
    i = pl.program_id(1)
    head = pl.program_id(0)
    k_scale, q_scale = scale_ref[head, 0], scale_ref[head, 1]

    def split8(x):
        hi = x.astype(F8)
        lo = (x - hi.astype(F32)).astype(F8)
        return hi, lo

    qt = jnp.clip(qt_ref[0, 0].astype(F32) * q_scale, -F8_MAX, F8_MAX)
    qh, ql = split8(qt)
    for mp in range(2):
        h, l = qh[mp * QK_DIM:(mp + 1) * QK_DIM], ql[mp * QK_DIM:(mp + 1) * QK_DIM]
        rq_ref[mp] = jnp.concatenate([h, h, l, l], axis=0)

    @pl.when(i == 0)
    def _():
        rows_per_step = 512

        def prep(t, carry):
            rows = pl.ds(pl.multiple_of(t * rows_per_step, rows_per_step), rows_per_step)
            kf = jnp.clip(k_ref[0, rows, :].astype(F32) * k_scale, -F8_MAX, F8_MAX)
            kh, kl = split8(kf)
            hu, lu = pltpu.bitcast(kh, jnp.uint32), pltpu.bitcast(kl, jnp.uint32)
            lane = lax.broadcasted_iota(jnp.int32, hu.shape, 1)
            a1 = jnp.where(lane < QK_DIM, hu, pltpu.roll(lu, QK_DIM, 1))
            a2 = jnp.where(lane < QK_DIM, pltpu.roll(hu, QK_DIM, 1), lu)
            ka_ref[0, rows, :] = pltpu.bitcast(a1, F8)
            ka_ref[1, rows, :] = pltpu.bitcast(a2, F8)
            return carry

        lax.fori_loop(0, k_ref.shape[1] // rows_per_step, prep, 0)

    m_ref[...] = jnp.full(m_ref.shape, NEG_INF, F32)
    acc_ref[...] = jnp.zeros(acc_ref.shape, F32)
    s_ref, cmax_ref = (s0_ref, s1_ref), (cmax0_ref, cmax1_ref)
    last = (i * bq) // bk
    ones = jnp.ones((ONES_ROWS, CHUNK), BF16)

    def scores_chunk(j, slot, r):
        rows = pl.ds(pl.multiple_of(j * bk + r * CHUNK, CHUNK), CHUNK)
        for mp in range(2):
            a = ka_ref[mp, rows, :]
            st = jnp.dot(jnp.concatenate([a, a], axis=1), rq_ref[mp], preferred_element_type=F32)
            s_ref[slot][r * CHUNK:(r + 1) * CHUNK, mp * bq:(mp + 1) * bq] = st
            cmax_ref[slot][r:r + 1, mp * bq:(mp + 1) * bq] = jnp.max(st, axis=0, keepdims=True)

    def softmax_chunk(slot, r, m_next):
        pt = jnp.exp2(s_ref[slot][r * CHUNK:(r + 1) * CHUNK, :] - m_next)
        p_ref[r * CHUNK:(r + 1) * CHUNK, :] = pt.astype(BF16)

    def values_chunk(j, r, pv):
        lhs = jnp.concatenate([vt_ref[0, j * n_chunks + r], ones], axis=0)
        d = jnp.dot(lhs, p_ref[r * CHUNK:(r + 1) * CHUNK, :], preferred_element_type=F32)
        return d if pv is None else pv + d

    def stage(j, par, next_block, used_chunks=n_chunks):
        m_prev = m_ref[...]
        m_next = jnp.maximum(m_prev, jnp.max(cmax_ref[par][0:used_chunks, :], axis=0, keepdims=True))
        alpha = jnp.exp2(m_prev - m_next)
        pv = None
        for r in range(n_chunks):
            if next_block is not None:
                scores_chunk(next_block, 1 - par, r)
            if r < used_chunks:
                softmax_chunk(par, r, m_next)
                pv = values_chunk(j, r, pv)
        m_ref[...] = m_next
        acc_ref[...] = alpha * acc_ref[...] + pv

    def body(j, carry):
        @pl.when(j % 2 == 0)
        def _():
            stage(j, 0, j + 1)

        @pl.when(j % 2 == 1)
        def _():
            stage(j, 1, j + 1)

        return carry

    def drain(par, first_masked):
        used = first_masked + bq // CHUNK
        for r in range(first_masked, used):
            st = s_ref[par][r * CHUNK:(r + 1) * CHUNK, :]
            key = r * CHUNK - first_masked * CHUNK + lax.broadcasted_iota(jnp.int32, st.shape, 0)
            col = lax.broadcasted_iota(jnp.int32, st.shape, 1)
            st = jnp.where(key <= jnp.where(col >= bq, col - bq, col), st, NEG_INF)
            s_ref[par][r * CHUNK:(r + 1) * CHUNK, :] = st
            cmax_ref[par][r:r + 1, :] = jnp.max(st, axis=0, keepdims=True)
        stage(last, par, None, used)

    for r in range(n_chunks):
        scores_chunk(0, 0, r)
    lax.fori_loop(0, last, body, 0)

    q_per_k = bk // bq
    for par in range(2):
        for pos in range(q_per_k):
            @pl.when(jnp.logical_and(last % 2 == par, i % q_per_k == pos))
            def _(par=par, pos=pos):
                drain(par, pos * (bq // CHUNK))

    lam = (jnp.exp(jnp.sum(lq1_ref[...] * lk1_ref[...], axis=-1, keepdims=True))
           - jnp.exp(jnp.sum(lq2_ref[...] * lk2_ref[...], axis=-1, keepdims=True)) + LAM_INIT)
    ot = acc_ref[0:HEAD, :] / acc_ref[HEAD:HEAD + 1, :]
    dt = ot[:, 0:bq] - lam * ot[:, bq:2 * bq]
    yt = dt * lax.rsqrt(jnp.mean(dt * dt, axis=0, keepdims=True) + NORM_EPS)
    o_ref[...] = (yt.T * g_ref[...] * (1.0 - LAM_INIT)).astype(BF16)


def _diff_attention(proj, qt, vt, lq1, lk1, lq2, lk2, g):
    s_len = proj.shape[1]
    assert DIFF_BK % DIFF_BQ == 0 and s_len % DIFF_BK == 0
    vec = lambda a: a.reshape(1, -1).astype(F32)
    small = lambda width: pl.BlockSpec((1, width), lambda h, i: (0, 0))
    k_max = jnp.max(jnp.abs(proj[N_HEADS:2 * N_HEADS].astype(F32)), axis=(1, 2))
    q_max = jnp.max(jnp.abs(qt.astype(F32)), axis=(1, 2, 3))
    ok = (k_max > 0) & (q_max > 0) & jnp.isfinite(k_max) & jnp.isfinite(q_max)
    shift = jnp.where(ok, jnp.round(0.5 * (jnp.log2(jnp.where(ok, q_max, 1.0)) - jnp.log2(jnp.where(ok, k_max, 1.0)))), 0.0)
    shift = shift.astype(jnp.int32)
    one = jnp.ones_like(k_max)
    scales = jnp.stack([jnp.ldexp(one, shift), jnp.ldexp(one, -shift)], axis=1)
    return pl.pallas_call(
        _diff_kernel,
        grid=(N_HEADS, s_len // DIFF_BQ),
        in_specs=[
            pl.BlockSpec(memory_space=pltpu.SMEM),
            small(QK_DIM), small(QK_DIM), small(QK_DIM), small(QK_DIM),
            pl.BlockSpec((1, 1, HEAD, DIFF_BQ), lambda h, i: (h, i, 0, 0)),
            pl.BlockSpec((1, s_len, HEAD), lambda h, i: (N_HEADS + h, 0, 0)),
            pl.BlockSpec((1, s_len // CHUNK, HEAD, CHUNK), lambda h, i: (h, 0, 0, 0)),
            small(HEAD),
        ],
        out_specs=pl.BlockSpec((DIFF_BQ, HEAD), lambda h, i: (i, h)),
        out_shape=jax.ShapeDtypeStruct((s_len, N_HEADS * HEAD), BF16),
        scratch_shapes=[
            pltpu.VMEM((2, 4 * QK_DIM, DIFF_BQ), F8),
            pltpu.VMEM((2, s_len, HEAD), F8),
            pltpu.VMEM((DIFF_BK, 2 * DIFF_BQ), F32),
            pltpu.VMEM((DIFF_BK, 2 * DIFF_BQ), F32),
            pltpu.VMEM((8, 2 * DIFF_BQ), F32),
            pltpu.VMEM((8, 2 * DIFF_BQ), F32),
            pltpu.VMEM((DIFF_BK, 2 * DIFF_BQ), BF16),
            pltpu.VMEM((1, 2 * DIFF_BQ), F32),
            pltpu.VMEM((HEAD + ONES_ROWS, 2 * DIFF_BQ), F32),
        ],
        compiler_params=_cparams(2),
        name="diff_attn",
    )(scales, vec(lq1), vec(lk1), vec(lq2), vec(lk2), qt, proj, vt, vec(g))


def _sb_kernel(q_ref, k_ref, v_ref, g_ref, o_ref, acc_ref, stick_ref, *, blk, group):
    scale = HEAD ** -0.5
    r = lax.broadcasted_iota(jnp.int32, (blk, blk), 0)
    c = lax.broadcasted_iota(jnp.int32, (blk, blk), 1)
    later = jnp.where(r > c, 1.0, 0.0).astype(BF16)
    strict = c < r

    def block(q, j, stick, masked):
        start = pl.multiple_of(j * blk, blk)
        kj = k_ref[0, pl.ds(start, blk), :]
        vj = v_ref[0, pl.ds(start, blk), :]
        z = lax.dot_general(q, kj, (((1,), (1,)), ((), ())), preferred_element_type=F32) * scale
        log_beta = jnp.minimum(z, 0.0) - jnp.log(1.0 + jnp.exp(-jnp.abs(z)))
        log_rest = log_beta - z
        if masked:
            log_rest = jnp.where(strict, log_rest, 0.0)
        hi = log_rest.astype(BF16)
        lo = (log_rest - hi.astype(F32)).astype(BF16)
        within = (jnp.dot(hi, later, preferred_element_type=F32)
                  + jnp.dot(lo, later, preferred_element_type=F32))
        a = jnp.exp(log_beta + within + stick)
        if masked:
            a = jnp.where(strict, a, 0.0)
        return (jnp.dot(a.astype(BF16), vj, preferred_element_type=F32),
                jnp.sum(log_rest, axis=1, keepdims=True))

    first = pl.program_id(1) * group
    for g in range(group):
        i = first + g
        q = q_ref[0, g * blk:(g + 1) * blk, :]
        pv_diag, spent_diag = block(q, i, 0.0, True)
        pv_prev, spent_prev = block(q, jnp.maximum(i - 1, 0), spent_diag, False)
        has_prev = i > 0
        acc_ref[g] = pv_diag + jnp.where(has_prev, pv_prev, 0.0)
        stick_ref[g] = jnp.broadcast_to(spent_diag + jnp.where(has_prev, spent_prev, 0.0), stick_ref.shape[1:])

    for g in range(group):
        q = q_ref[0, g * blk:(g + 1) * blk, :]

        def alive():
            return jnp.max(stick_ref[g]) > SB_DEAD_LOG

        def cond(state):
            j, go = state
            return jnp.logical_and(j >= 0, go)

        def body(state):
            j, _ = state
            pv, spent = block(q, j, stick_ref[g, :, 0:1], False)
            acc_ref[g] += pv
            stick_ref[g] += spent
            return j - 1, alive()

        lax.while_loop(cond, body, (first + g - 2, alive()))
        o_ref[g * blk:(g + 1) * blk, :] = (_rms(acc_ref[g]) * g_ref[...]).astype(BF16)


def _sb_attention(proj, g, blk=256, group=4):
    s_len = proj.shape[1]
    rows = blk * group
    return pl.pallas_call(
        functools.partial(_sb_kernel, blk=blk, group=group),
        grid=(N_HEADS, s_len // rows),
        in_specs=[
            pl.BlockSpec((1, rows, HEAD), lambda h, i: (3 * N_HEADS + h, i, 0)),
            pl.BlockSpec((1, s_len, HEAD), lambda h, i: (4 * N_HEADS + h, 0, 0)),
            pl.BlockSpec((1, s_len, HEAD), lambda h, i: (5 * N_HEADS + h, 0, 0)),
            pl.BlockSpec((1, HEAD), lambda h, i: (0, 0)),
        ],
        out_specs=pl.BlockSpec((rows, HEAD), lambda h, i: (i, h)),
        out_shape=jax.ShapeDtypeStruct((s_len, N_HEADS * HEAD), BF16),
        scratch_shapes=[
            pltpu.VMEM((group, blk, HEAD), F32),
            pltpu.VMEM((group, blk, HEAD), F32),
        ],
        compiler_params=_cparams(2),
        name="sb_attn",
    )(proj, proj, proj, g.reshape(1, HEAD).astype(F32))


def _outproj_kernel(md_ref, ms_ref, x_ref, w_ref, g_ref, x1_ref, h2_ref):
    half = md_ref.shape[1]
    y = (jnp.dot(md_ref[...], w_ref[0:half, :], preferred_element_type=F32)
         + jnp.dot(ms_ref[...], w_ref[half:2 * half, :], preferred_element_type=F32))
    x1 = x_ref[...] + y
    x1_ref[...] = x1
    h2_ref[...] = (_rms(x1) * g_ref[...]).astype(BF16)


def _out_project(mixed_diff, mixed_sb, x2, w_out_bf16, ln2, tm=512):
    s_len = x2.shape[0]
    half = N_HEADS * HEAD
    return pl.pallas_call(
        _outproj_kernel,
        grid=(s_len // tm,),
        in_specs=[
            pl.BlockSpec((tm, half), lambda m: (m, 0)),
            pl.BlockSpec((tm, half), lambda m: (m, 0)),
            pl.BlockSpec((tm, D_MODEL), lambda m: (m, 0)),
            pl.BlockSpec((2 * half, D_MODEL), lambda m: (0, 0)),
            pl.BlockSpec((1, D_MODEL), lambda m: (0, 0)),
        ],
        out_specs=[
            pl.BlockSpec((tm, D_MODEL), lambda m: (m, 0)),
            pl.BlockSpec((tm, D_MODEL), lambda m: (m, 0)),
        ],
        out_shape=[
            jax.ShapeDtypeStruct((s_len, D_MODEL), F32),
            jax.ShapeDtypeStruct((s_len, D_MODEL), BF16),
        ],
        compiler_params=_cparams(1),
        name="proj_out",
    )(mixed_diff, mixed_sb, x2, w_out_bf16, ln2.reshape(1, D_MODEL))


def _mlp_kernel(h_ref, w1_ref, w2_ref, x1_ref, g_ref, o_ref, acc_ref):
    f = pl.program_id(1)

    @pl.when(f == 0)
    def _():
        acc_ref[...] = x1_ref[...]

    a = jnp.dot(h_ref[...], w1_ref[...], preferred_element_type=F32)
    a = jnp.square(jnp.maximum(a, 0.0)).astype(BF16)
    acc_ref[...] += jnp.dot(a, w2_ref[...], preferred_element_type=F32)

    @pl.when(f == pl.num_programs(1) - 1)
    def _():
        o_ref[...] = _rms(acc_ref[...]) * g_ref[...]


def _mlp(h2, x1, w1_bf16, w2_bf16, ln_f, tm=512, tf=1024):
    s_len = h2.shape[0]
    return pl.pallas_call(
        _mlp_kernel,
        grid=(s_len // tm, D_FF // tf),
        in_specs=[
            pl.BlockSpec((tm, D_MODEL), lambda m, f: (m, 0)),
            pl.BlockSpec((D_MODEL, tf), lambda m, f: (0, f)),
            pl.BlockSpec((tf, D_MODEL), lambda m, f: (f, 0)),
            pl.BlockSpec((tm, D_MODEL), lambda m, f: (m, 0)),
            pl.BlockSpec((1, D_MODEL), lambda m, f: (0, 0)),
        ],
        out_specs=pl.BlockSpec((tm, D_MODEL), lambda m, f: (m, 0)),
        out_shape=jax.ShapeDtypeStruct((s_len, D_MODEL), F32),
        scratch_shapes=[pltpu.VMEM((tm, D_MODEL), F32)],
        compiler_params=_cparams(2),
        name="mlp",
    )(h2, w1_bf16, w2_bf16, x1, ln_f.reshape(1, D_MODEL))


def kernel(x, ln1, w_in, lambda_q1, lambda_k1, lambda_q2, lambda_k2, diff_head_norm, sb_head_norm,
           w_out, ln2, w_mlp_in, w_mlp_out, ln_f):
    b, s_len, _ = x.shape
    assert b == 1 and ln1.shape[0] == 1
    x2 = x.reshape(s_len, D_MODEL)
    proj, qt, vt = _project(x2, ln1[0], w_in[0].astype(BF16))
    mixed_diff = _diff_attention(proj, qt, vt, lambda_q1[0], lambda_k1[0], lambda_q2[0], lambda_k2[0], diff_head_norm[0])
    mixed_sb = _sb_attention(proj, sb_head_norm[0])
    x1, h2 = _out_project(mixed_diff, mixed_sb, x2, w_out[0].astype(BF16), ln2[0])
    out = _mlp(h2, x1, w_mlp_in[0].astype(BF16), w_mlp_out[0].astype(BF16), ln_f)
    return out.reshape(b, s_len, D_MODEL)
```

```python
import functools
import math

import jax
import jax.numpy as jnp
import numpy as np
from jax import lax
from jax.experimental import pallas as pl
from jax.experimental.pallas import tpu as pltpu

F32 = jnp.float32
BF16 = jnp.bfloat16

D_MODEL = 2048
HEAD = 128
N_HEADS = 8
QK_DIM = 64
D_FF = 4 * D_MODEL
PROJ_WIDTH = 6 * N_HEADS * HEAD
ROT_COLS = 2 * N_HEADS * HEAD
ROT_DIMS = QK_DIM // 4
ROPE_THETA = 500000.0
NORM_EPS = 1e-6
NEG_INF = -1e30
LAM_INIT = 0.8 - 0.6 * math.exp(-0.3 * 0)
SB_DEAD_LOG = -105.0
DIFF_BQ = 512
DIFF_BK = 1024
ONES_ROWS = 16
CHUNK = 256
LOG2E = math.log2(math.e)
DIFF_QSCALE = QK_DIM ** -0.5 * LOG2E

VMEM_LIMIT = 56 * 1024 * 1024


def _cparams(n_axes, flags=None):
    return pltpu.CompilerParams(dimension_semantics=("arbitrary",) * n_axes,
                                vmem_limit_bytes=VMEM_LIMIT, flags=flags)


def _rms(x):
    return x * lax.rsqrt(jnp.mean(x * x, axis=-1, keepdims=True) + NORM_EPS)


def _proj_kernel(x_ref, g_ref, w_ref, cos_ref, sup_ref, sdn_ref, o_ref, qt_ref, vt_ref, h_ref, y_ref, *, tn):
    n = pl.program_id(1)
    n_heads_tile = tn // HEAD
    assert n_heads_tile == N_HEADS

    @pl.when(n == 0)
    def _():
        h_ref[...] = (_rms(x_ref[...]) * g_ref[...]).astype(BF16)

    def project(with_rotary):
        acc = jnp.dot(h_ref[...], w_ref[...], preferred_element_type=F32)
        for c in range(n_heads_tile):
            y = acc[:, c * HEAD:(c + 1) * HEAD]
            if with_rotary:
                y = (y * cos_ref[...] + pltpu.roll(y, ROT_DIMS // 2, 1) * sup_ref[...]
                     + pltpu.roll(y, HEAD - ROT_DIMS // 2, 1) * sdn_ref[...])
            o_ref[c] = y.astype(BF16)
            y_ref[:, c * HEAD:(c + 1) * HEAD] = y

    @pl.when(n < ROT_COLS // tn)
    def _():
        project(True)

    @pl.when(n >= ROT_COLS // tn)
    def _():
        project(False)

    def store_transposed(dst_ref, c, scale):
        blocks, _, width = dst_ref.shape[1:]
        for b in range(blocks):
            y = y_ref[b * width:(b + 1) * width, c * HEAD:(c + 1) * HEAD]
            dst_ref[c, b] = (y * scale).T.astype(BF16)

    @pl.when(n == 0)
    def _():
        for c in range(n_heads_tile):
            store_transposed(qt_ref, c, DIFF_QSCALE)

    @pl.when(n == 2)
    def _():
        for c in range(n_heads_tile):
            store_transposed(vt_ref, c, 1.0)


def _rotary_tables(s_len):
    half = ROT_DIMS // 2
    inv_freq = ROPE_THETA ** (-jnp.arange(0, ROT_DIMS, 2, dtype=F32) / ROT_DIMS)
    comp = np.arange(HEAD) % QK_DIM
    first, second = comp < half, (comp >= half) & (comp < ROT_DIMS)
    ang = jnp.arange(s_len, dtype=F32)[:, None] * inv_freq[comp % half][None, :]
    cos, sin = jnp.cos(ang), jnp.sin(ang)
    cos_t = jnp.where(first | second, cos, 1.0)
    sup_t = jnp.where(second, sin, 0.0)
    sdn_t = jnp.where(first, -sin, 0.0)
    return cos_t, sup_t, sdn_t


def _project(x2, ln1, w_in_bf16, tm=1024, tn=N_HEADS * HEAD):
    s_len = x2.shape[0]
    cos, sup, sdn = _rotary_tables(s_len)
    kern = functools.partial(_proj_kernel, tn=tn)
    return pl.pallas_call(
        kern,
        grid=(s_len // tm, PROJ_WIDTH // tn),
        in_specs=[
            pl.BlockSpec((tm, D_MODEL), lambda m, n: (m, 0)),
            pl.BlockSpec((1, D_MODEL), lambda m, n: (0, 0)),
            pl.BlockSpec((D_MODEL, tn), lambda m, n: (0, n)),
            pl.BlockSpec((tm, HEAD), lambda m, n: (m, 0)),
            pl.BlockSpec((tm, HEAD), lambda m, n: (m, 0)),
            pl.BlockSpec((tm, HEAD), lambda m, n: (m, 0)),
        ],
        out_specs=[
            pl.BlockSpec((tn // HEAD, tm, HEAD), lambda m, n: (n, m, 0)),
            pl.BlockSpec((N_HEADS, tm // DIFF_BQ, HEAD, DIFF_BQ), lambda m, n: (0, m, 0, 0)),
            pl.BlockSpec((N_HEADS, tm // CHUNK, HEAD, CHUNK), lambda m, n: (0, m, 0, 0)),
        ],
        out_shape=[
            jax.ShapeDtypeStruct((PROJ_WIDTH // HEAD, s_len, HEAD), BF16),
            jax.ShapeDtypeStruct((N_HEADS, s_len // DIFF_BQ, HEAD, DIFF_BQ), BF16),
            jax.ShapeDtypeStruct((N_HEADS, s_len // CHUNK, HEAD, CHUNK), BF16),
        ],
        scratch_shapes=[
            pltpu.VMEM((tm, D_MODEL), BF16),
            pltpu.VMEM((tm, tn), F32),
        ],
        compiler_params=_cparams(2),
        name="proj_in",
    )(x2, ln1.reshape(1, D_MODEL), w_in_bf16, cos, sup, sdn)


def _diff_kernel(lq1_ref, lk1_ref, lq2_ref, lk2_ref, qt_ref, k_ref, vt_ref, g_ref, o_ref,
                 qqt_ref, s0_ref, s1_ref, cmax0_ref, cmax1_ref, m_ref, acc_ref):
    bq, bk = DIFF_BQ, DIFF_BK
    n_chunks = bk // CHUNK
    i = pl.program_id(1)
    qt = qt_ref[0, 0]
    sub = lax.broadcasted_iota(jnp.int32, (HEAD, bq), 0)
    zero = jnp.zeros_like(qt)
    qqt_ref[:, 0:bq] = jnp.where(sub < QK_DIM, qt, zero)
    qqt_ref[:, bq:2 * bq] = jnp.where(sub >= QK_DIM, qt, zero)
    m_ref[...] = jnp.full(m_ref.shape, NEG_INF, F32)
    acc_ref[...] = jnp.zeros(acc_ref.shape, F32)
    s_ref, cmax_ref = (s0_ref, s1_ref), (cmax0_ref, cmax1_ref)
    last = (i * bq) // bk
    ones = jnp.ones((ONES_ROWS, CHUNK), BF16)

    def scores_chunk(j, slot, r):
        rows = pl.ds(pl.multiple_of(j * bk + r * CHUNK, CHUNK), CHUNK)
        st = jnp.dot(k_ref[0, rows, :], qqt_ref[...], preferred_element_type=F32)
        s_ref[slot][r * CHUNK:(r + 1) * CHUNK, :] = st
        cmax_ref[slot][r:r + 1, :] = jnp.max(st, axis=0, keepdims=True)

    def softmax_chunk(slot, r, m_next):
        return jnp.exp2(s_ref[slot][r * CHUNK:(r + 1) * CHUNK, :] - m_next).astype(BF16)

    def values_chunk(j, r, pt, pv):
        lhs = jnp.concatenate([vt_ref[0, j * n_chunks + r], ones], axis=0)
        d = jnp.dot(lhs, pt, preferred_element_type=F32)
        return d if pv is None else pv + d

    def stage(j, par, next_block, used_chunks=n_chunks):
        m_prev = m_ref[...]
        m_next = jnp.maximum(m_prev, jnp.max(cmax_ref[par][0:used_chunks, :], axis=0, keepdims=True))
        alpha = jnp.exp2(m_prev - m_next)
        pv = None
        for r in range(n_chunks):
            if next_block is not None:
                scores_chunk(next_block, 1 - par, r)
            if r < used_chunks:
                pv = values_chunk(j, r, softmax_chunk(par, r, m_next), pv)
        m_ref[...] = m_next
        acc_ref[...] = alpha * acc_ref[...] + pv

    def body(j, carry):
        @pl.when(j % 2 == 0)
        def _():
            stage(j, 0, j + 1)

        @pl.when(j % 2 == 1)
        def _():
            stage(j, 1, j + 1)

        return carry

    def drain(par, first_masked):
        used = first_masked + bq // CHUNK
        for r in range(first_masked, used):
            st = s_ref[par][r * CHUNK:(r + 1) * CHUNK, :]
            key = r * CHUNK - first_masked * CHUNK + lax.broadcasted_iota(jnp.int32, st.shape, 0)
            col = lax.broadcasted_iota(jnp.int32, st.shape, 1)
            st = jnp.where(key <= jnp.where(col >= bq, col - bq, col), st, NEG_INF)
            s_ref[par][r * CHUNK:(r + 1) * CHUNK, :] = st
            cmax_ref[par][r:r + 1, :] = jnp.max(st, axis=0, keepdims=True)
        stage(last, par, None, used)

    for r in range(n_chunks):
        scores_chunk(0, 0, r)
    lax.fori_loop(0, last, body, 0)

    q_per_k = bk // bq
    for par in range(2):
        for pos in range(q_per_k):
            @pl.when(jnp.logical_and(last % 2 == par, i % q_per_k == pos))
            def _(par=par, pos=pos):
                drain(par, pos * (bq // CHUNK))

    lam = (jnp.exp(jnp.sum(lq1_ref[...] * lk1_ref[...], axis=-1, keepdims=True))
           - jnp.exp(jnp.sum(lq2_ref[...] * lk2_ref[...], axis=-1, keepdims=True)) + LAM_INIT)
    ot = acc_ref[0:HEAD, :] / acc_ref[HEAD:HEAD + 1, :]
    dt = ot[:, 0:bq] - lam * ot[:, bq:2 * bq]
    yt = dt * lax.rsqrt(jnp.mean(dt * dt, axis=0, keepdims=True) + NORM_EPS)
    o_ref[...] = (yt.T * g_ref[...] * (1.0 - LAM_INIT)).astype(BF16)


def _diff_attention(proj, qt, vt, lq1, lk1, lq2, lk2, g):
    s_len = proj.shape[1]
    assert DIFF_BK % DIFF_BQ == 0 and s_len % DIFF_BK == 0
    vec = lambda a: a.reshape(1, -1).astype(F32)
    small = lambda width: pl.BlockSpec((1, width), lambda h, i: (0, 0))
    return pl.pallas_call(
        _diff_kernel,
        grid=(N_HEADS, s_len // DIFF_BQ),
        in_specs=[
            small(QK_DIM), small(QK_DIM), small(QK_DIM), small(QK_DIM),
            pl.BlockSpec((1, 1, HEAD, DIFF_BQ), lambda h, i: (h, i, 0, 0)),
            pl.BlockSpec((1, s_len, HEAD), lambda h, i: (N_HEADS + h, 0, 0)),
            pl.BlockSpec((1, s_len // CHUNK, HEAD, CHUNK), lambda h, i: (h, 0, 0, 0)),
            small(HEAD),
        ],
        out_specs=pl.BlockSpec((DIFF_BQ, HEAD), lambda h, i: (i, h)),
        out_shape=jax.ShapeDtypeStruct((s_len, N_HEADS * HEAD), BF16),
        scratch_shapes=[
            pltpu.VMEM((HEAD, 2 * DIFF_BQ), BF16),
            pltpu.VMEM((DIFF_BK, 2 * DIFF_BQ), F32),
            pltpu.VMEM((DIFF_BK, 2 * DIFF_BQ), F32),
            pltpu.VMEM((8, 2 * DIFF_BQ), F32),
            pltpu.VMEM((8, 2 * DIFF_BQ), F32),
            pltpu.VMEM((1, 2 * DIFF_BQ), F32),
            pltpu.VMEM((HEAD + ONES_ROWS, 2 * DIFF_BQ), F32),
        ],
        compiler_params=_cparams(2),
        name="diff_attn",
    )(vec(lq1), vec(lk1), vec(lq2), vec(lk2), qt, proj, vt, vec(g))


def _sb_kernel(q_ref, k_ref, v_ref, g_ref, o_ref, acc_ref, stick_ref, *, blk, group):
    scale = HEAD ** -0.5
    r = lax.broadcasted_iota(jnp.int32, (blk, blk), 0)
    c = lax.broadcasted_iota(jnp.int32, (blk, blk), 1)
    later = jnp.where(r > c, 1.0, 0.0).astype(BF16)
    strict = c < r

    def block(q, j, stick, masked):
        start = pl.multiple_of(j * blk, blk)
        kj = k_ref[0, pl.ds(start, blk), :]
        vj = v_ref[0, pl.ds(start, blk), :]
        z = lax.dot_general(q, kj, (((1,), (1,)), ((), ())), preferred_element_type=F32) * scale
        log_beta = jnp.minimum(z, 0.0) - jnp.log(1.0 + jnp.exp(-jnp.abs(z)))
        log_rest = log_beta - z
        if masked:
            log_rest = jnp.where(strict, log_rest, 0.0)
        hi = log_rest.astype(BF16)
        lo = (log_rest - hi.astype(F32)).astype(BF16)
        within = (jnp.dot(hi, later, preferred_element_type=F32)
                  + jnp.dot(lo, later, preferred_element_type=F32))
        a = jnp.exp(log_beta + within + stick)
        if masked:
            a = jnp.where(strict, a, 0.0)
        return (jnp.dot(a.astype(BF16), vj, preferred_element_type=F32),
                jnp.sum(log_rest, axis=1, keepdims=True))

    first = pl.program_id(1) * group
    for g in range(group):
        i = first + g
        q = q_ref[0, g * blk:(g + 1) * blk, :]
        pv_diag, spent_diag = block(q, i, 0.0, True)
        pv_prev, spent_prev = block(q, jnp.maximum(i - 1, 0), spent_diag, False)
        has_prev = i > 0
        acc_ref[g] = pv_diag + jnp.where(has_prev, pv_prev, 0.0)
        stick_ref[g] = jnp.broadcast_to(spent_diag + jnp.where(has_prev, spent_prev, 0.0), stick_ref.shape[1:])

    for g in range(group):
        q = q_ref[0, g * blk:(g + 1) * blk, :]

        def alive():
            return jnp.max(stick_ref[g]) > SB_DEAD_LOG

        def cond(state):
            j, go = state
            return jnp.logical_and(j >= 0, go)

        def body(state):
            j, _ = state
            pv, spent = block(q, j, stick_ref[g, :, 0:1], False)
            acc_ref[g] += pv
            stick_ref[g] += spent
            return j - 1, alive()

        lax.while_loop(cond, body, (first + g - 2, alive()))
        o_ref[g * blk:(g + 1) * blk, :] = (_rms(acc_ref[g]) * g_ref[...]).astype(BF16)


def _sb_attention(proj, g, blk=256, group=4):
    s_len = proj.shape[1]
    rows = blk * group
    return pl.pallas_call(
        functools.partial(_sb_kernel, blk=blk, group=group),
        grid=(N_HEADS, s_len // rows),
        in_specs=[
            pl.BlockSpec((1, rows, HEAD), lambda h, i: (3 * N_HEADS + h, i, 0)),
            pl.BlockSpec((1, s_len, HEAD), lambda h, i: (4 * N_HEADS + h, 0, 0)),
            pl.BlockSpec((1, s_len, HEAD), lambda h, i: (5 * N_HEADS + h, 0, 0)),
            pl.BlockSpec((1, HEAD), lambda h, i: (0, 0)),
        ],
        out_specs=pl.BlockSpec((rows, HEAD), lambda h, i: (i, h)),
        out_shape=jax.ShapeDtypeStruct((s_len, N_HEADS * HEAD), BF16),
        scratch_shapes=[
            pltpu.VMEM((group, blk, HEAD), F32),
            pltpu.VMEM((group, blk, HEAD), F32),
        ],
        compiler_params=_cparams(2),
        name="sb_attn",
    )(proj, proj, proj, g.reshape(1, HEAD).astype(F32))


def _outproj_kernel(md_ref, ms_ref, x_ref, w_ref, g_ref, x1_ref, h2_ref):
    half = md_ref.shape[1]
    y = (jnp.dot(md_ref[...], w_ref[0:half, :], preferred_element_type=F32)
         + jnp.dot(ms_ref[...], w_ref[half:2 * half, :], preferred_element_type=F32))
    x1 = x_ref[...] + y
    x1_ref[...] = x1
    h2_ref[...] = (_rms(x1) * g_ref[...]).astype(BF16)


def _out_project(mixed_diff, mixed_sb, x2, w_out_bf16, ln2, tm=512):
    s_len = x2.shape[0]
    half = N_HEADS * HEAD
    return pl.pallas_call(
        _outproj_kernel,
        grid=(s_len // tm,),
        in_specs=[
            pl.BlockSpec((tm, half), lambda m: (m, 0)),
            pl.BlockSpec((tm, half), lambda m: (m, 0)),
            pl.BlockSpec((tm, D_MODEL), lambda m: (m, 0)),
            pl.BlockSpec((2 * half, D_MODEL), lambda m: (0, 0)),
            pl.BlockSpec((1, D_MODEL), lambda m: (0, 0)),
        ],
        out_specs=[
            pl.BlockSpec((tm, D_MODEL), lambda m: (m, 0)),
            pl.BlockSpec((tm, D_MODEL), lambda m: (m, 0)),
        ],
        out_shape=[
            jax.ShapeDtypeStruct((s_len, D_MODEL), F32),
            jax.ShapeDtypeStruct((s_len, D_MODEL), BF16),
        ],
        compiler_params=_cparams(1),
        name="proj_out",
    )(mixed_diff, mixed_sb, x2, w_out_bf16, ln2.reshape(1, D_MODEL))


def _mlp_kernel(h_ref, w1_ref, w2_ref, x1_ref, g_ref, o_ref, acc_ref):
    f = pl.program_id(1)

    @pl.when(f == 0)
    def _():
        acc_ref[...] = x1_ref[...]

    a = jnp.dot(h_ref[...], w1_ref[...], preferred_element_type=F32)
    a = jnp.square(jnp.maximum(a, 0.0)).astype(BF16)
    acc_ref[...] += jnp.dot(a, w2_ref[...], preferred_element_type=F32)

    @pl.when(f == pl.num_programs(1) - 1)
    def _():
        o_ref[...] = _rms(acc_ref[...]) * g_ref[...]


def _mlp(h2, x1, w1_bf16, w2_bf16, ln_f, tm=512, tf=1024):
    s_len = h2.shape[0]
    return pl.pallas_call(
        _mlp_kernel,
        grid=(s_len // tm, D_FF // tf),
        in_specs=[
            pl.BlockSpec((tm, D_MODEL), lambda m, f: (m, 0)),
            pl.BlockSpec((D_MODEL, tf), lambda m, f: (0, f)),
            pl.BlockSpec((tf, D_MODEL), lambda m, f: (f, 0)),
            pl.BlockSpec((tm, D_MODEL), lambda m, f: (m, 0)),
            pl.BlockSpec((1, D_MODEL), lambda m, f: (0, 0)),
        ],
        out_specs=pl.BlockSpec((tm, D_MODEL), lambda m, f: (m, 0)),
        out_shape=jax.ShapeDtypeStruct((s_len, D_MODEL), F32),
        scratch_shapes=[pltpu.VMEM((tm, D_MODEL), F32)],
        compiler_params=_cparams(2),
        name="mlp",
    )(h2, w1_bf16, w2_bf16, x1, ln_f.reshape(1, D_MODEL))


def kernel(x, ln1, w_in, lambda_q1, lambda_k1, lambda_q2, lambda_k2, diff_head_norm, sb_head_norm,
           w_out, ln2, w_mlp_in, w_mlp_out, ln_f):
    b, s_len, _ = x.shape
    assert b == 1 and ln1.shape[0] == 1
    x2 = x.reshape(s_len, D_MODEL)
    proj, qt, vt = _project(x2, ln1[0], w_in[0].astype(BF16))
    mixed_diff = _diff_attention(proj, qt, vt, lambda_q1[0], lambda_k1[0], lambda_q2[0], lambda_k2[0], diff_head_norm[0])
    mixed_sb = _sb_attention(proj, sb_head_norm[0])
    x1, h2 = _out_project(mixed_diff, mixed_sb, x2, w_out[0].astype(BF16), ln2[0])
    out = _mlp(h2, x1, w_mlp_in[0].astype(BF16), w_mlp_out[0].astype(BF16), ln_f)
    return out.reshape(b, s_len, D_MODEL)
```

```python
import functools
import math

import jax
import jax.numpy as jnp
import numpy as np
from jax import lax
from jax.experimental import pallas as pl
from jax.experimental.pallas import tpu as pltpu

F32 = jnp.float32
BF16 = jnp.bfloat16

D_MODEL = 2048
HEAD = 128
N_HEADS = 8
QK_DIM = 64
D_FF = 4 * D_MODEL
PROJ_WIDTH = 6 * N_HEADS * HEAD
ROT_COLS = 2 * N_HEADS * HEAD
ROT_DIMS = QK_DIM // 4
ROPE_THETA = 500000.0
NORM_EPS = 1e-6
NEG_INF = -1e30
LAM_INIT = 0.8 - 0.6 * math.exp(-0.3 * 0)
SB_DEAD_LOG = -105.0
DIFF_BQ = 512
DIFF_BK = 1024
ONES_ROWS = 16
CHUNK = 256
LOG2E = math.log2(math.e)
DIFF_QSCALE = QK_DIM ** -0.5 * LOG2E

VMEM_LIMIT = 56 * 1024 * 1024


def _cparams(n_axes, flags=None):
    return pltpu.CompilerParams(dimension_semantics=("arbitrary",) * n_axes,
                                vmem_limit_bytes=VMEM_LIMIT, flags=flags)


def _rms(x):
    return x * lax.rsqrt(jnp.mean(x * x, axis=-1, keepdims=True) + NORM_EPS)


def _proj_kernel(x_ref, g_ref, w_ref, cos_ref, sup_ref, sdn_ref, o_ref, qt_ref, vt_ref, h_ref, y_ref, *, tn):
    n = pl.program_id(1)
    n_heads_tile = tn // HEAD
    assert n_heads_tile == N_HEADS

    @pl.when(n == 0)
    def _():
        h_ref[...] = (_rms(x_ref[...]) * g_ref[...]).astype(BF16)

    def project(with_rotary):
        acc = jnp.dot(h_ref[...], w_ref[...], preferred_element_type=F32)
        for c in range(n_heads_tile):
            y = acc[:, c * HEAD:(c + 1) * HEAD]
            if with_rotary:
                y = (y * cos_ref[...] + pltpu.roll(y, ROT_DIMS // 2, 1) * sup_ref[...]
                     + pltpu.roll(y, HEAD - ROT_DIMS // 2, 1) * sdn_ref[...])
            o_ref[c] = y.astype(BF16)
            y_ref[:, c * HEAD:(c + 1) * HEAD] = y

    @pl.when(n < ROT_COLS // tn)
    def _():
        project(True)

    @pl.when(n >= ROT_COLS // tn)
    def _():
        project(False)

    def store_transposed(dst_ref, c, scale):
        blocks, _, width = dst_ref.shape[1:]
        for b in range(blocks):
            y = y_ref[b * width:(b + 1) * width, c * HEAD:(c + 1) * HEAD]
            dst_ref[c, b] = (y * scale).T.astype(BF16)

    @pl.when(n == 0)
    def _():
        for c in range(n_heads_tile):
            store_transposed(qt_ref, c, DIFF_QSCALE)

    @pl.when(n == 2)
    def _():
        for c in range(n_heads_tile):
            store_transposed(vt_ref, c, 1.0)


def _rotary_tables(s_len):
    half = ROT_DIMS // 2
    inv_freq = ROPE_THETA ** (-jnp.arange(0, ROT_DIMS, 2, dtype=F32) / ROT_DIMS)
    comp = np.arange(HEAD) % QK_DIM
    first, second = comp < half, (comp >= half) & (comp < ROT_DIMS)
    ang = jnp.arange(s_len, dtype=F32)[:, None] * inv_freq[comp % half][None, :]
    cos, sin = jnp.cos(ang), jnp.sin(ang)
    cos_t = jnp.where(first | second, cos, 1.0)
    sup_t = jnp.where(second, sin, 0.0)
    sdn_t = jnp.where(first, -sin, 0.0)
    return cos_t, sup_t, sdn_t


def _project(x2, ln1, w_in_bf16, tm=1024, tn=N_HEADS * HEAD):
    s_len = x2.shape[0]
    cos, sup, sdn = _rotary_tables(s_len)
    kern = functools.partial(_proj_kernel, tn=tn)
    return pl.pallas_call(
        kern,
        grid=(s_len // tm, PROJ_WIDTH // tn),
        in_specs=[
            pl.BlockSpec((tm, D_MODEL), lambda m, n: (m, 0)),
            pl.BlockSpec((1, D_MODEL), lambda m, n: (0, 0)),
            pl.BlockSpec((D_MODEL, tn), lambda m, n: (0, n)),
            pl.BlockSpec((tm, HEAD), lambda m, n: (m, 0)),
            pl.BlockSpec((tm, HEAD), lambda m, n: (m, 0)),
            pl.BlockSpec((tm, HEAD), lambda m, n: (m, 0)),
        ],
        out_specs=[
            pl.BlockSpec((tn // HEAD, tm, HEAD), lambda m, n: (n, m, 0)),
            pl.BlockSpec((N_HEADS, tm // DIFF_BQ, HEAD, DIFF_BQ), lambda m, n: (0, m, 0, 0)),
            pl.BlockSpec((N_HEADS, tm // CHUNK, HEAD, CHUNK), lambda m, n: (0, m, 0, 0)),
        ],
        out_shape=[
            jax.ShapeDtypeStruct((PROJ_WIDTH // HEAD, s_len, HEAD), BF16),
            jax.ShapeDtypeStruct((N_HEADS, s_len // DIFF_BQ, HEAD, DIFF_BQ), BF16),
            jax.ShapeDtypeStruct((N_HEADS, s_len // CHUNK, HEAD, CHUNK), BF16),
        ],
        scratch_shapes=[
            pltpu.VMEM((tm, D_MODEL), BF16),
            pltpu.VMEM((tm, tn), F32),
        ],
        compiler_params=_cparams(2),
        name="proj_in",
    )(x2, ln1.reshape(1, D_MODEL), w_in_bf16, cos, sup, sdn)


def _diff_kernel(lq1_ref, lk1_ref, lq2_ref, lk2_ref, qt_ref, k_ref, vt_ref, g_ref, o_ref,
                 qqt_ref, s0_ref, s1_ref, cmax0_ref, cmax1_ref, m_ref, acc_ref):
    bq, bk = DIFF_BQ, DIFF_BK
    n_chunks = bk // CHUNK
    i = pl.program_id(1)
    qt = qt_ref[0, 0]
    sub = lax.broadcasted_iota(jnp.int32, (HEAD, bq), 0)
    zero = jnp.zeros_like(qt)
    qqt_ref[:, 0:bq] = jnp.where(sub < QK_DIM, qt, zero)
    qqt_ref[:, bq:2 * bq] = jnp.where(sub >= QK_DIM, qt, zero)
    m_ref[...] = jnp.full(m_ref.shape, NEG_INF, F32)
    acc_ref[...] = jnp.zeros(acc_ref.shape, F32)
    s_ref, cmax_ref = (s0_ref, s1_ref), (cmax0_ref, cmax1_ref)
    last = (i * bq) // bk
    ones = jnp.ones((ONES_ROWS, CHUNK), BF16)

    def scores_chunk(j, slot, r):
        rows = pl.ds(pl.multiple_of(j * bk + r * CHUNK, CHUNK), CHUNK)
        st = jnp.dot(k_ref[0, rows, :], qqt_ref[...], preferred_element_type=F32)
        s_ref[slot][r * CHUNK:(r + 1) * CHUNK, :] = st
        cmax_ref[slot][r:r + 1, :] = jnp.max(st, axis=0, keepdims=True)

    def softmax_chunk(slot, r, m_next):
        return jnp.exp2((s_ref[slot][r * CHUNK:(r + 1) * CHUNK, :] - m_next).astype(BF16))

    def values_chunk(j, r, pt, pv):
        lhs = jnp.concatenate([vt_ref[0, j * n_chunks + r], ones], axis=0)
        d = jnp.dot(lhs, pt, preferred_element_type=F32)
        return d if pv is None else pv + d

    def stage(j, par, next_block, used_chunks=n_chunks):
        m_prev = m_ref[...]
        m_next = jnp.maximum(m_prev, jnp.max(cmax_ref[par][0:used_chunks, :], axis=0, keepdims=True))
        alpha = jnp.exp2(m_prev - m_next)
        pv = None
        for r in range(n_chunks):
            if next_block is not None:
                scores_chunk(next_block, 1 - par, r)
            if r < used_chunks:
                pv = values_chunk(j, r, softmax_chunk(par, r, m_next), pv)
        m_ref[...] = m_next
        acc_ref[...] = alpha * acc_ref[...] + pv

    def body(j, carry):
        @pl.when(j % 2 == 0)
        def _():
            stage(j, 0, j + 1)

        @pl.when(j % 2 == 1)
        def _():
            stage(j, 1, j + 1)

        return carry

    def drain(par, first_masked):
        used = first_masked + bq // CHUNK
        for r in range(first_masked, used):
            st = s_ref[par][r * CHUNK:(r + 1) * CHUNK, :]
            key = r * CHUNK - first_masked * CHUNK + lax.broadcasted_iota(jnp.int32, st.shape, 0)
            col = lax.broadcasted_iota(jnp.int32, st.shape, 1)
            st = jnp.where(key <= jnp.where(col >= bq, col - bq, col), st, NEG_INF)
            s_ref[par][r * CHUNK:(r + 1) * CHUNK, :] = st
            cmax_ref[par][r:r + 1, :] = jnp.max(st, axis=0, keepdims=True)
        stage(last, par, None, used)

    for r in range(n_chunks):
        scores_chunk(0, 0, r)
    lax.fori_loop(0, last, body, 0)

    q_per_k = bk // bq
    for par in range(2):
        for pos in range(q_per_k):
            @pl.when(jnp.logical_and(last % 2 == par, i % q_per_k == pos))
            def _(par=par, pos=pos):
                drain(par, pos * (bq // CHUNK))

    lam = (jnp.exp(jnp.sum(lq1_ref[...] * lk1_ref[...], axis=-1, keepdims=True))
           - jnp.exp(jnp.sum(lq2_ref[...] * lk2_ref[...], axis=-1, keepdims=True)) + LAM_INIT)
    ot = acc_ref[0:HEAD, :] / acc_ref[HEAD:HEAD + 1, :]
    dt = ot[:, 0:bq] - lam * ot[:, bq:2 * bq]
    yt = dt * lax.rsqrt(jnp.mean(dt * dt, axis=0, keepdims=True) + NORM_EPS)
    o_ref[...] = (yt.T * g_ref[...] * (1.0 - LAM_INIT)).astype(BF16)


def _diff_attention(proj, qt, vt, lq1, lk1, lq2, lk2, g):
    s_len = proj.shape[1]
    assert DIFF_BK % DIFF_BQ == 0 and s_len % DIFF_BK == 0
    vec = lambda a: a.reshape(1, -1).astype(F32)
    small = lambda width: pl.BlockSpec((1, width), lambda h, i: (0, 0))
    return pl.pallas_call(
        _diff_kernel,
        grid=(N_HEADS, s_len // DIFF_BQ),
        in_specs=[
            small(QK_DIM), small(QK_DIM), small(QK_DIM), small(QK_DIM),
            pl.BlockSpec((1, 1, HEAD, DIFF_BQ), lambda h, i: (h, i, 0, 0)),
            pl.BlockSpec((1, s_len, HEAD), lambda h, i: (N_HEADS + h, 0, 0)),
            pl.BlockSpec((1, s_len // CHUNK, HEAD, CHUNK), lambda h, i: (h, 0, 0, 0)),
            small(HEAD),
        ],
        out_specs=pl.BlockSpec((DIFF_BQ, HEAD), lambda h, i: (i, h)),
        out_shape=jax.ShapeDtypeStruct((s_len, N_HEADS * HEAD), BF16),
        scratch_shapes=[
            pltpu.VMEM((HEAD, 2 * DIFF_BQ), BF16),
            pltpu.VMEM((DIFF_BK, 2 * DIFF_BQ), F32),
            pltpu.VMEM((DIFF_BK, 2 * DIFF_BQ), F32),
            pltpu.VMEM((8, 2 * DIFF_BQ), F32),
            pltpu.VMEM((8, 2 * DIFF_BQ), F32),
            pltpu.VMEM((1, 2 * DIFF_BQ), F32),
            pltpu.VMEM((HEAD + ONES_ROWS, 2 * DIFF_BQ), F32),
        ],
        compiler_params=_cparams(2),
        name="diff_attn",
    )(vec(lq1), vec(lk1), vec(lq2), vec(lk2), qt, proj, vt, vec(g))


def _sb_kernel(q_ref, k_ref, v_ref, g_ref, o_ref, acc_ref, stick_ref, *, blk, group):
    scale = HEAD ** -0.5
    r = lax.broadcasted_iota(jnp.int32, (blk, blk), 0)
    c = lax.broadcasted_iota(jnp.int32, (blk, blk), 1)
    later = jnp.where(r > c, 1.0, 0.0).astype(BF16)
    strict = c < r

    def block(q, j, stick, masked):
        start = pl.multiple_of(j * blk, blk)
        kj = k_ref[0, pl.ds(start, blk), :]
        vj = v_ref[0, pl.ds(start, blk), :]
        z = lax.dot_general(q, kj, (((1,), (1,)), ((), ())), preferred_element_type=F32) * scale
        log_beta = jnp.minimum(z, 0.0) - jnp.log(1.0 + jnp.exp(-jnp.abs(z)))
        log_rest = log_beta - z
        if masked:
            log_rest = jnp.where(strict, log_rest, 0.0)
        hi = log_rest.astype(BF16)
        lo = (log_rest - hi.astype(F32)).astype(BF16)
        within = (jnp.dot(hi, later, preferred_element_type=F32)
                  + jnp.dot(lo, later, preferred_element_type=F32))
        a = jnp.exp(log_beta + within + stick)
        if masked:
            a = jnp.where(strict, a, 0.0)
        return (jnp.dot(a.astype(BF16), vj, preferred_element_type=F32),
                jnp.sum(log_rest, axis=1, keepdims=True))

    first = pl.program_id(1) * group
    for g in range(group):
        i = first + g
        q = q_ref[0, g * blk:(g + 1) * blk, :]
        pv_diag, spent_diag = block(q, i, 0.0, True)
        pv_prev, spent_prev = block(q, jnp.maximum(i - 1, 0), spent_diag, False)
        has_prev = i > 0
        acc_ref[g] = pv_diag + jnp.where(has_prev, pv_prev, 0.0)
        stick_ref[g] = jnp.broadcast_to(spent_diag + jnp.where(has_prev, spent_prev, 0.0), stick_ref.shape[1:])

    for g in range(group):
        q = q_ref[0, g * blk:(g + 1) * blk, :]

        def alive():
            return jnp.max(stick_ref[g]) > SB_DEAD_LOG

        def cond(state):
            j, go = state
            return jnp.logical_and(j >= 0, go)

        def body(state):
            j, _ = state
            pv, spent = block(q, j, stick_ref[g, :, 0:1], False)
            acc_ref[g] += pv
            stick_ref[g] += spent
            return j - 1, alive()

        lax.while_loop(cond, body, (first + g - 2, alive()))
        o_ref[g * blk:(g + 1) * blk, :] = (_rms(acc_ref[g]) * g_ref[...]).astype(BF16)


def _sb_attention(proj, g, blk=256, group=4):
    s_len = proj.shape[1]
    rows = blk * group
    return pl.pallas_call(
        functools.partial(_sb_kernel, blk=blk, group=group),
        grid=(N_HEADS, s_len // rows),
        in_specs=[
            pl.BlockSpec((1, rows, HEAD), lambda h, i: (3 * N_HEADS + h, i, 0)),
            pl.BlockSpec((1, s_len, HEAD), lambda h, i: (4 * N_HEADS + h, 0, 0)),
            pl.BlockSpec((1, s_len, HEAD), lambda h, i: (5 * N_HEADS + h, 0, 0)),
            pl.BlockSpec((1, HEAD), lambda h, i: (0, 0)),
        ],
        out_specs=pl.BlockSpec((rows, HEAD), lambda h, i: (i, h)),
        out_shape=jax.ShapeDtypeStruct((s_len, N_HEADS * HEAD), BF16),
        scratch_shapes=[
            pltpu.VMEM((group, blk, HEAD), F32),
            pltpu.VMEM((group, blk, HEAD), F32),
        ],
        compiler_params=_cparams(2),
        name="sb_attn",
    )(proj, proj, proj, g.reshape(1, HEAD).astype(F32))


def _outproj_kernel(md_ref, ms_ref, x_ref, w_ref, g_ref, x1_ref, h2_ref):
    half = md_ref.shape[1]
    y = (jnp.dot(md_ref[...], w_ref[0:half, :], preferred_element_type=F32)
         + jnp.dot(ms_ref[...], w_ref[half:2 * half, :], preferred_element_type=F32))
    x1 = x_ref[...] + y
    x1_ref[...] = x1
    h2_ref[...] = (_rms(x1) * g_ref[...]).astype(BF16)


def _out_project(mixed_diff, mixed_sb, x2, w_out_bf16, ln2, tm=512):
    s_len = x2.shape[0]
    half = N_HEADS * HEAD
    return pl.pallas_call(
        _outproj_kernel,
        grid=(s_len // tm,),
        in_specs=[
            pl.BlockSpec((tm, half), lambda m: (m, 0)),
            pl.BlockSpec((tm, half), lambda m: (m, 0)),
            pl.BlockSpec((tm, D_MODEL), lambda m: (m, 0)),
            pl.BlockSpec((2 * half, D_MODEL), lambda m: (0, 0)),
            pl.BlockSpec((1, D_MODEL), lambda m: (0, 0)),
        ],
        out_specs=[
            pl.BlockSpec((tm, D_MODEL), lambda m: (m, 0)),
            pl.BlockSpec((tm, D_MODEL), lambda m: (m, 0)),
        ],
        out_shape=[
            jax.ShapeDtypeStruct((s_len, D_MODEL), F32),
            jax.ShapeDtypeStruct((s_len, D_MODEL), BF16),
        ],
        compiler_params=_cparams(1),
        name="proj_out",
    )(mixed_diff, mixed_sb, x2, w_out_bf16, ln2.reshape(1, D_MODEL))


def _mlp_kernel(h_ref, w1_ref, w2_ref, x1_ref, g_ref, o_ref, acc_ref):
    f = pl.program_id(1)

    @pl.when(f == 0)
    def _():
        acc_ref[...] = x1_ref[...]

    a = jnp.dot(h_ref[...], w1_ref[...], preferred_element_type=F32)
    a = jnp.square(jnp.maximum(a, 0.0)).astype(BF16)
    acc_ref[...] += jnp.dot(a, w2_ref[...], preferred_element_type=F32)

    @pl.when(f == pl.num_programs(1) - 1)
    def _():
        o_ref[...] = _rms(acc_ref[...]) * g_ref[...]


def _mlp(h2, x1, w1_bf16, w2_bf16, ln_f, tm=512, tf=1024):
    s_len = h2.shape[0]
    return pl.pallas_call(
        _mlp_kernel,
        grid=(s_len // tm, D_FF // tf),
        in_specs=[
            pl.BlockSpec((tm, D_MODEL), lambda m, f: (m, 0)),
            pl.BlockSpec((D_MODEL, tf), lambda m, f: (0, f)),
            pl.BlockSpec((tf, D_MODEL), lambda m, f: (f, 0)),
            pl.BlockSpec((tm, D_MODEL), lambda m, f: (m, 0)),
            pl.BlockSpec((1, D_MODEL), lambda m, f: (0, 0)),
        ],
        out_specs=pl.BlockSpec((tm, D_MODEL), lambda m, f: (m, 0)),
        out_shape=jax.ShapeDtypeStruct((s_len, D_MODEL), F32),
        scratch_shapes=[pltpu.VMEM((tm, D_MODEL), F32)],
        compiler_params=_cparams(2),
        name="mlp",
    )(h2, w1_bf16, w2_bf16, x1, ln_f.reshape(1, D_MODEL))


def kernel(x, ln1, w_in, lambda_q1, lambda_k1, lambda_q2, lambda_k2, diff_head_norm, sb_head_norm,
           w_out, ln2, w_mlp_in, w_mlp_out, ln_f):
    b, s_len, _ = x.shape
    assert b == 1 and ln1.shape[0] == 1
    x2 = x.reshape(s_len, D_MODEL)
    proj, qt, vt = _project(x2, ln1[0], w_in[0].astype(BF16))
    mixed_diff = _diff_attention(proj, qt, vt, lambda_q1[0], lambda_k1[0], lambda_q2[0], lambda_k2[0], diff_head_norm[0])
    mixed_sb = _sb_attention(proj, sb_head_norm[0])
    x1, h2 = _out_project(mixed_diff, mixed_sb, x2, w_out[0].astype(BF16), ln2[0])
    out = _mlp(h2, x1, w_mlp_in[0].astype(BF16), w_mlp_out[0].astype(BF16), ln_f)
    return out.reshape(b, s_len, D_MODEL)
```

```python
import functools
import math

import jax
import jax.numpy as jnp
import numpy as np
from jax import lax
from jax.experimental import pallas as pl
from jax.experimental.pallas import tpu as pltpu

F32 = jnp.float32
BF16 = jnp.bfloat16

D_MODEL = 2048
HEAD = 128
N_HEADS = 8
QK_DIM = 64
D_FF = 4 * D_MODEL
PROJ_WIDTH = 6 * N_HEADS * HEAD
ROT_COLS = 2 * N_HEADS * HEAD
ROT_DIMS = QK_DIM // 4
ROPE_THETA = 500000.0
NORM_EPS = 1e-6
NEG_INF = -1e30
LAM_INIT = 0.8 - 0.6 * math.exp(-0.3 * 0)
SB_DEAD_LOG = -105.0
DIFF_BQ = 512
DIFF_BK = 1024
ONES_ROWS = 16
CHUNK = 256
LOG2E = math.log2(math.e)
DIFF_QSCALE = QK_DIM ** -0.5 * LOG2E

VMEM_LIMIT = 56 * 1024 * 1024


def _cparams(n_axes, flags=None):
    return pltpu.CompilerParams(dimension_semantics=("arbitrary",) * n_axes,
                                vmem_limit_bytes=VMEM_LIMIT, flags=flags)


def _rms(x):
    return x * lax.rsqrt(jnp.mean(x * x, axis=-1, keepdims=True) + NORM_EPS)


def _proj_kernel(x_ref, g_ref, w_ref, cos_ref, sup_ref, sdn_ref, o_ref, qt_ref, vt_ref, h_ref, y_ref, *, tn):
    n = pl.program_id(1)
    n_heads_tile = tn // HEAD
    assert n_heads_tile == N_HEADS

    @pl.when(n == 0)
    def _():
        h_ref[...] = (_rms(x_ref[...]) * g_ref[...]).astype(BF16)

    def project(with_rotary):
        acc = jnp.dot(h_ref[...], w_ref[...], preferred_element_type=F32)
        for c in range(n_heads_tile):
            y = acc[:, c * HEAD:(c + 1) * HEAD]
            if with_rotary:
                y = (y * cos_ref[...] + pltpu.roll(y, ROT_DIMS // 2, 1) * sup_ref[...]
                     + pltpu.roll(y, HEAD - ROT_DIMS // 2, 1) * sdn_ref[...])
            o_ref[c] = y.astype(BF16)
            y_ref[:, c * HEAD:(c + 1) * HEAD] = y

    @pl.when(n < ROT_COLS // tn)
    def _():
        project(True)

    @pl.when(n >= ROT_COLS // tn)
    def _():
        project(False)

    def store_transposed(dst_ref, c, scale):
        blocks, _, width = dst_ref.shape[1:]
        for b in range(blocks):
            y = y_ref[b * width:(b + 1) * width, c * HEAD:(c + 1) * HEAD]
            dst_ref[c, b] = (y * scale).T.astype(BF16)

    @pl.when(n == 0)
    def _():
        for c in range(n_heads_tile):
            store_transposed(qt_ref, c, DIFF_QSCALE)

    @pl.when(n == 2)
    def _():
        for c in range(n_heads_tile):
            store_transposed(vt_ref, c, 1.0)


def _rotary_tables(s_len):
    half = ROT_DIMS // 2
    inv_freq = ROPE_THETA ** (-jnp.arange(0, ROT_DIMS, 2, dtype=F32) / ROT_DIMS)
    comp = np.arange(HEAD) % QK_DIM
    first, second = comp < half, (comp >= half) & (comp < ROT_DIMS)
    ang = jnp.arange(s_len, dtype=F32)[:, None] * inv_freq[comp % half][None, :]
    cos, sin = jnp.cos(ang), jnp.sin(ang)
    cos_t = jnp.where(first | second, cos, 1.0)
    sup_t = jnp.where(second, sin, 0.0)
    sdn_t = jnp.where(first, -sin, 0.0)
    return cos_t, sup_t, sdn_t


def _project(x2, ln1, w_in_bf16, tm=1024, tn=N_HEADS * HEAD):
    s_len = x2.shape[0]
    cos, sup, sdn = _rotary_tables(s_len)
    kern = functools.partial(_proj_kernel, tn=tn)
    return pl.pallas_call(
        kern,
        grid=(s_len // tm, PROJ_WIDTH // tn),
        in_specs=[
            pl.BlockSpec((tm, D_MODEL), lambda m, n: (m, 0)),
            pl.BlockSpec((1, D_MODEL), lambda m, n: (0, 0)),
            pl.BlockSpec((D_MODEL, tn), lambda m, n: (0, n)),
            pl.BlockSpec((tm, HEAD), lambda m, n: (m, 0)),
            pl.BlockSpec((tm, HEAD), lambda m, n: (m, 0)),
            pl.BlockSpec((tm, HEAD), lambda m, n: (m, 0)),
        ],
        out_specs=[
            pl.BlockSpec((tn // HEAD, tm, HEAD), lambda m, n: (n, m, 0)),
            pl.BlockSpec((N_HEADS, tm // DIFF_BQ, HEAD, DIFF_BQ), lambda m, n: (0, m, 0, 0)),
            pl.BlockSpec((N_HEADS, tm // CHUNK, HEAD, CHUNK), lambda m, n: (0, m, 0, 0)),
        ],
        out_shape=[
            jax.ShapeDtypeStruct((PROJ_WIDTH // HEAD, s_len, HEAD), BF16),
            jax.ShapeDtypeStruct((N_HEADS, s_len // DIFF_BQ, HEAD, DIFF_BQ), BF16),
            jax.ShapeDtypeStruct((N_HEADS, s_len // CHUNK, HEAD, CHUNK), BF16),
        ],
        scratch_shapes=[
            pltpu.VMEM((tm, D_MODEL), BF16),
            pltpu.VMEM((tm, tn), F32),
        ],
        compiler_params=_cparams(2),
        name="proj_in",
    )(x2, ln1.reshape(1, D_MODEL), w_in_bf16, cos, sup, sdn)


def _diff_kernel(lq1_ref, lk1_ref, lq2_ref, lk2_ref, qt_ref, k_ref, vt_ref, g_ref, o_ref,
                 qqt_ref, s0_ref, s1_ref, cmax0_ref, cmax1_ref, m_ref, acc_ref):
    bq, bk = DIFF_BQ, DIFF_BK
    n_chunks = bk // CHUNK
    i = pl.program_id(1)
    qt = qt_ref[0, 0]
    sub = lax.broadcasted_iota(jnp.int32, (HEAD, bq), 0)
    zero = jnp.zeros_like(qt)
    qqt_ref[:, 0:bq] = jnp.where(sub < QK_DIM, qt, zero)
    qqt_ref[:, bq:2 * bq] = jnp.where(sub >= QK_DIM, qt, zero)
    m_ref[...] = jnp.full(m_ref.shape, NEG_INF, F32)
    acc_ref[...] = jnp.zeros(acc_ref.shape, F32)
    s_ref, cmax_ref = (s0_ref, s1_ref), (cmax0_ref, cmax1_ref)
    last = (i * bq) // bk
    ones = jnp.ones((ONES_ROWS, CHUNK), BF16)

    def scores_chunk(j, slot, r):
        rows = pl.ds(pl.multiple_of(j * bk + r * CHUNK, CHUNK), CHUNK)
        st = jnp.dot(k_ref[0, rows, :], qqt_ref[...], preferred_element_type=F32)
        s_ref[slot][r * CHUNK:(r + 1) * CHUNK, :] = st
        cmax_ref[slot][r:r + 1, :] = jnp.max(st, axis=0, keepdims=True)

    def softmax_chunk(slot, r, m_next):
        return jnp.exp2((s_ref[slot][r * CHUNK:(r + 1) * CHUNK, :] - m_next).astype(BF16))

    def values_chunk(j, r, pt, pv):
        lhs = jnp.concatenate([vt_ref[0, j * n_chunks + r], ones], axis=0)
        d = jnp.dot(lhs, pt, preferred_element_type=F32)
        return d if pv is None else pv + d

    def stage(j, par, next_block, used_chunks=n_chunks):
        m_prev = m_ref[...]
        m_next = jnp.maximum(m_prev, jnp.max(cmax_ref[par][0:used_chunks, :], axis=0, keepdims=True))
        alpha = jnp.exp2(m_prev - m_next)
        pv = None
        for r in range(n_chunks):
            if next_block is not None:
                scores_chunk(next_block, 1 - par, r)
            if r < used_chunks:
                pv = values_chunk(j, r, softmax_chunk(par, r, m_next), pv)
        m_ref[...] = m_next
        acc_ref[...] = alpha * acc_ref[...] + pv

    def body(j, carry):
        @pl.when(j % 2 == 0)
        def _():
            stage(j, 0, j + 1)

        @pl.when(j % 2 == 1)
        def _():
            stage(j, 1, j + 1)

        return carry

    def drain(par, first_masked):
        used = first_masked + bq // CHUNK
        for r in range(first_masked, used):
            st = s_ref[par][r * CHUNK:(r + 1) * CHUNK, :]
            key = r * CHUNK - first_masked * CHUNK + lax.broadcasted_iota(jnp.int32, st.shape, 0)
            col = lax.broadcasted_iota(jnp.int32, st.shape, 1)
            st = jnp.where(key <= jnp.where(col >= bq, col - bq, col), st, NEG_INF)
            s_ref[par][r * CHUNK:(r + 1) * CHUNK, :] = st
            cmax_ref[par][r:r + 1, :] = jnp.max(st, axis=0, keepdims=True)
        stage(last, par, None, used)

    for r in range(n_chunks):
        scores_chunk(0, 0, r)
    lax.fori_loop(0, last, body, 0)

    q_per_k = bk // bq
    for par in range(2):
        for pos in range(q_per_k):
            @pl.when(jnp.logical_and(last % 2 == par, i % q_per_k == pos))
            def _(par=par, pos=pos):
                drain(par, pos * (bq // CHUNK))

    lam = (jnp.exp(jnp.sum(lq1_ref[...] * lk1_ref[...], axis=-1, keepdims=True))
           - jnp.exp(jnp.sum(lq2_ref[...] * lk2_ref[...], axis=-1, keepdims=True)) + LAM_INIT)
    ot = acc_ref[0:HEAD, :] / acc_ref[HEAD:HEAD + 1, :]
    dt = ot[:, 0:bq] - lam * ot[:, bq:2 * bq]
    yt = dt * lax.rsqrt(jnp.mean(dt * dt, axis=0, keepdims=True) + NORM_EPS)
    o_ref[...] = (yt.T * g_ref[...] * (1.0 - LAM_INIT)).astype(BF16)


def _diff_attention(proj, qt, vt, lq1, lk1, lq2, lk2, g):
    s_len = proj.shape[1]
    assert DIFF_BK % DIFF_BQ == 0 and s_len % DIFF_BK == 0
    vec = lambda a: a.reshape(1, -1).astype(F32)
    small = lambda width: pl.BlockSpec((1, width), lambda h, i: (0, 0))
    return pl.pallas_call(
        _diff_kernel,
        grid=(N_HEADS, s_len // DIFF_BQ),
        in_specs=[
            small(QK_DIM), small(QK_DIM), small(QK_DIM), small(QK_DIM),
            pl.BlockSpec((1, 1, HEAD, DIFF_BQ), lambda h, i: (h, i, 0, 0)),
            pl.BlockSpec((1, s_len, HEAD), lambda h, i: (N_HEADS + h, 0, 0)),
            pl.BlockSpec((1, s_len // CHUNK, HEAD, CHUNK), lambda h, i: (h, 0, 0, 0)),
            small(HEAD),
        ],
        out_specs=pl.BlockSpec((DIFF_BQ, HEAD), lambda h, i: (i, h)),
        out_shape=jax.ShapeDtypeStruct((s_len, N_HEADS * HEAD), BF16),
        scratch_shapes=[
            pltpu.VMEM((HEAD, 2 * DIFF_BQ), BF16),
            pltpu.VMEM((DIFF_BK, 2 * DIFF_BQ), F32),
            pltpu.VMEM((DIFF_BK, 2 * DIFF_BQ), F32),
            pltpu.VMEM((8, 2 * DIFF_BQ), F32),
            pltpu.VMEM((8, 2 * DIFF_BQ), F32),
            pltpu.VMEM((1, 2 * DIFF_BQ), F32),
            pltpu.VMEM((HEAD + ONES_ROWS, 2 * DIFF_BQ), F32),
        ],
        compiler_params=_cparams(2),
        name="diff_attn",
    )(vec(lq1), vec(lk1), vec(lq2), vec(lk2), qt, proj, vt, vec(g))


def _sb_kernel(q_ref, k_ref, v_ref, g_ref, o_ref, acc_ref, stick_ref, *, blk, group):
    scale = HEAD ** -0.5
    r = lax.broadcasted_iota(jnp.int32, (blk, blk), 0)
    c = lax.broadcasted_iota(jnp.int32, (blk, blk), 1)
    later = jnp.where(r > c, 1.0, 0.0).astype(BF16)
    strict = c < r

    def logits(q, j):
        start = pl.multiple_of(j * blk, blk)
        return lax.dot_general(q, k_ref[0, pl.ds(start, blk), :], (((1,), (1,)), ((), ())),
                               preferred_element_type=F32) * scale

    def log_masses(z, masked):
        log_beta = jnp.minimum(z, 0.0) - jnp.log(1.0 + jnp.exp(-jnp.abs(z)))
        log_rest = log_beta - z
        if masked:
            log_rest = jnp.where(strict, log_rest, 0.0)
        hi = log_rest.astype(BF16)
        lo = (log_rest - hi.astype(F32)).astype(BF16)
        within = (jnp.dot(hi, later, preferred_element_type=F32)
                  + jnp.dot(lo, later, preferred_element_type=F32))
        return log_beta + within, jnp.sum(log_rest, axis=1, keepdims=True)

    def values(log_a, stick, j, masked):
        a = jnp.exp(log_a + stick)
        if masked:
            a = jnp.where(strict, a, 0.0)
        start = pl.multiple_of(j * blk, blk)
        return jnp.dot(a.astype(BF16), v_ref[0, pl.ds(start, blk), :], preferred_element_type=F32)

    def block(q, j, stick, masked):
        log_a, spent = log_masses(logits(q, j), masked)
        return values(log_a, stick, j, masked), spent

    first = pl.program_id(1) * group
    qs = [q_ref[0, g * blk:(g + 1) * blk, :] for g in range(group)]
    blocks = ([(g, first + g, True) for g in range(group)]
              + [(g, jnp.maximum(first + g - 1, 0), False) for g in range(group)])
    zs = [logits(qs[g], j) for g, j, _ in blocks]
    masses = [log_masses(z, masked) for z, (_, _, masked) in zip(zs, blocks)]
    for g in range(group):
        i = first + g
        (log_a_diag, spent_diag), (log_a_prev, spent_prev) = masses[g], masses[group + g]
        pv_diag = values(log_a_diag, 0.0, i, True)
        pv_prev = values(log_a_prev, spent_diag, jnp.maximum(i - 1, 0), False)
        has_prev = i > 0
        acc_ref[g] = pv_diag + jnp.where(has_prev, pv_prev, 0.0)
        stick_ref[g] = jnp.broadcast_to(spent_diag + jnp.where(has_prev, spent_prev, 0.0), stick_ref.shape[1:])

    for g in range(group):
        q = q_ref[0, g * blk:(g + 1) * blk, :]

        def alive():
            return jnp.max(stick_ref[g]) > SB_DEAD_LOG

        def cond(state):
            j, go = state
            return jnp.logical_and(j >= 0, go)

        def body(state):
            j, _ = state
            pv, spent = block(q, j, stick_ref[g, :, 0:1], False)
            acc_ref[g] += pv
            stick_ref[g] += spent
            return j - 1, alive()

        lax.while_loop(cond, body, (first + g - 2, alive()))
        o_ref[g * blk:(g + 1) * blk, :] = (_rms(acc_ref[g]) * g_ref[...]).astype(BF16)


def _sb_attention(proj, g, blk=256, group=4):
    s_len = proj.shape[1]
    rows = blk * group
    return pl.pallas_call(
        functools.partial(_sb_kernel, blk=blk, group=group),
        grid=(N_HEADS, s_len // rows),
        in_specs=[
            pl.BlockSpec((1, rows, HEAD), lambda h, i: (3 * N_HEADS + h, i, 0)),
            pl.BlockSpec((1, s_len, HEAD), lambda h, i: (4 * N_HEADS + h, 0, 0)),
            pl.BlockSpec((1, s_len, HEAD), lambda h, i: (5 * N_HEADS + h, 0, 0)),
            pl.BlockSpec((1, HEAD), lambda h, i: (0, 0)),
        ],
        out_specs=pl.BlockSpec((rows, HEAD), lambda h, i: (i, h)),
        out_shape=jax.ShapeDtypeStruct((s_len, N_HEADS * HEAD), BF16),
        scratch_shapes=[
            pltpu.VMEM((group, blk, HEAD), F32),
            pltpu.VMEM((group, blk, HEAD), F32),
        ],
        compiler_params=_cparams(2),
        name="sb_attn",
    )(proj, proj, proj, g.reshape(1, HEAD).astype(F32))


def _outproj_kernel(md_ref, ms_ref, x_ref, w_ref, g_ref, x1_ref, h2_ref):
    half = md_ref.shape[1]
    y = (jnp.dot(md_ref[...], w_ref[0:half, :], preferred_element_type=F32)
         + jnp.dot(ms_ref[...], w_ref[half:2 * half, :], preferred_element_type=F32))
    x1 = x_ref[...] + y
    x1_ref[...] = x1
    h2_ref[...] = (_rms(x1) * g_ref[...]).astype(BF16)


def _out_project(mixed_diff, mixed_sb, x2, w_out_bf16, ln2, tm=512):
    s_len = x2.shape[0]
    half = N_HEADS * HEAD
    return pl.pallas_call(
        _outproj_kernel,
        grid=(s_len // tm,),
        in_specs=[
            pl.BlockSpec((tm, half), lambda m: (m, 0)),
            pl.BlockSpec((tm, half), lambda m: (m, 0)),
            pl.BlockSpec((tm, D_MODEL), lambda m: (m, 0)),
            pl.BlockSpec((2 * half, D_MODEL), lambda m: (0, 0)),
            pl.BlockSpec((1, D_MODEL), lambda m: (0, 0)),
        ],
        out_specs=[
            pl.BlockSpec((tm, D_MODEL), lambda m: (m, 0)),
            pl.BlockSpec((tm, D_MODEL), lambda m: (m, 0)),
        ],
        out_shape=[
            jax.ShapeDtypeStruct((s_len, D_MODEL), F32),
            jax.ShapeDtypeStruct((s_len, D_MODEL), BF16),
        ],
        compiler_params=_cparams(1),
        name="proj_out",
    )(mixed_diff, mixed_sb, x2, w_out_bf16, ln2.reshape(1, D_MODEL))


def _mlp_kernel(h_ref, w1_ref, w2_ref, x1_ref, g_ref, o_ref, acc_ref):
    f = pl.program_id(1)

    @pl.when(f == 0)
    def _():
        acc_ref[...] = x1_ref[...]

    a = jnp.dot(h_ref[...], w1_ref[...], preferred_element_type=F32)
    a = jnp.square(jnp.maximum(a, 0.0)).astype(BF16)
    acc_ref[...] += jnp.dot(a, w2_ref[...], preferred_element_type=F32)

    @pl.when(f == pl.num_programs(1) - 1)
    def _():
        o_ref[...] = _rms(acc_ref[...]) * g_ref[...]


def _mlp(h2, x1, w1_bf16, w2_bf16, ln_f, tm=512, tf=1024):
    s_len = h2.shape[0]
    return pl.pallas_call(
        _mlp_kernel,
        grid=(s_len // tm, D_FF // tf),
        in_specs=[
            pl.BlockSpec((tm, D_MODEL), lambda m, f: (m, 0)),
            pl.BlockSpec((D_MODEL, tf), lambda m, f: (0, f)),
            pl.BlockSpec((tf, D_MODEL), lambda m, f: (f, 0)),
            pl.BlockSpec((tm, D_MODEL), lambda m, f: (m, 0)),
            pl.BlockSpec((1, D_MODEL), lambda m, f: (0, 0)),
        ],
        out_specs=pl.BlockSpec((tm, D_MODEL), lambda m, f: (m, 0)),
        out_shape=jax.ShapeDtypeStruct((s_len, D_MODEL), F32),
        scratch_shapes=[pltpu.VMEM((tm, D_MODEL), F32)],
        compiler_params=_cparams(2),
        name="mlp",
    )(h2, w1_bf16, w2_bf16, x1, ln_f.reshape(1, D_MODEL))


def kernel(x, ln1, w_in, lambda_q1, lambda_k1, lambda_q2, lambda_k2, diff_head_norm, sb_head_norm,
           w_out, ln2, w_mlp_in, w_mlp_out, ln_f):
    b, s_len, _ = x.shape
    assert b == 1 and ln1.shape[0] == 1
    x2 = x.reshape(s_len, D_MODEL)
    proj, qt, vt = _project(x2, ln1[0], w_in[0].astype(BF16))
    mixed_diff = _diff_attention(proj, qt, vt, lambda_q1[0], lambda_k1[0], lambda_q2[0], lambda_k2[0], diff_head_norm[0])
    mixed_sb = _sb_attention(proj, sb_head_norm[0])
    x1, h2 = _out_project(mixed_diff, mixed_sb, x2, w_out[0].astype(BF16), ln2[0])
    out = _mlp(h2, x1, w_mlp_in[0].astype(BF16), w_mlp_out[0].astype(BF16), ln_f)
    return out.reshape(b, s_len, D_MODEL)
```

```python
import functools
import math

import jax
import jax.numpy as jnp
import numpy as np
from jax import lax
from jax.experimental import pallas as pl
from jax.experimental.pallas import tpu as pltpu

F32 = jnp.float32
BF16 = jnp.bfloat16

D_MODEL = 2048
HEAD = 128
N_HEADS = 8
QK_DIM = 64
D_FF = 4 * D_MODEL
PROJ_WIDTH = 6 * N_HEADS * HEAD
ROT_COLS = 2 * N_HEADS * HEAD
ROT_DIMS = QK_DIM // 4
ROPE_THETA = 500000.0
NORM_EPS = 1e-6
NEG_INF = -1e30
LAM_INIT = 0.8 - 0.6 * math.exp(-0.3 * 0)
SB_DEAD_LOG = -105.0
DIFF_BQ = 512
DIFF_BK = 1024
ONES_ROWS = 16
CHUNK = 256
LOG2E = math.log2(math.e)
DIFF_QSCALE = QK_DIM ** -0.5 * LOG2E

VMEM_LIMIT = 56 * 1024 * 1024


def _cparams(n_axes, flags=None):
    return pltpu.CompilerParams(dimension_semantics=("arbitrary",) * n_axes,
                                vmem_limit_bytes=VMEM_LIMIT, flags=flags)


def _rms(x):
    return x * lax.rsqrt(jnp.mean(x * x, axis=-1, keepdims=True) + NORM_EPS)


def _proj_kernel(x_ref, g_ref, w_ref, cos_ref, sup_ref, sdn_ref, o_ref, qt_ref, vt_ref, h_ref, y_ref, *, tn):
    n = pl.program_id(1)
    n_heads_tile = tn // HEAD
    assert n_heads_tile == N_HEADS

    @pl.when(n == 0)
    def _():
        h_ref[...] = (_rms(x_ref[...]) * g_ref[...]).astype(BF16)

    def project(with_rotary):
        acc = jnp.dot(h_ref[...], w_ref[...], preferred_element_type=F32)
        for c in range(n_heads_tile):
            y = acc[:, c * HEAD:(c + 1) * HEAD]
            if with_rotary:
                y = (y * cos_ref[...] + pltpu.roll(y, ROT_DIMS // 2, 1) * sup_ref[...]
                     + pltpu.roll(y, HEAD - ROT_DIMS // 2, 1) * sdn_ref[...])
            o_ref[c] = y.astype(BF16)
            y_ref[:, c * HEAD:(c + 1) * HEAD] = y

    @pl.when(n < ROT_COLS // tn)
    def _():
        project(True)

    @pl.when(n >= ROT_COLS // tn)
    def _():
        project(False)

    def store_transposed(dst_ref, c, scale):
        blocks, _, width = dst_ref.shape[1:]
        for b in range(blocks):
            y = y_ref[b * width:(b + 1) * width, c * HEAD:(c + 1) * HEAD]
            dst_ref[c, b] = (y * scale).T.astype(BF16)

    @pl.when(n == 0)
    def _():
        for c in range(n_heads_tile):
            store_transposed(qt_ref, c, DIFF_QSCALE)

    @pl.when(n == 2)
    def _():
        for c in range(n_heads_tile):
            store_transposed(vt_ref, c, 1.0)


def _rotary_tables(s_len):
    half = ROT_DIMS // 2
    inv_freq = ROPE_THETA ** (-jnp.arange(0, ROT_DIMS, 2, dtype=F32) / ROT_DIMS)
    comp = np.arange(HEAD) % QK_DIM
    first, second = comp < half, (comp >= half) & (comp < ROT_DIMS)
    ang = jnp.arange(s_len, dtype=F32)[:, None] * inv_freq[comp % half][None, :]
    cos, sin = jnp.cos(ang), jnp.sin(ang)
    cos_t = jnp.where(first | second, cos, 1.0)
    sup_t = jnp.where(second, sin, 0.0)
    sdn_t = jnp.where(first, -sin, 0.0)
    return cos_t, sup_t, sdn_t


def _project(x2, ln1, w_in_bf16, tm=1024, tn=N_HEADS * HEAD):
    s_len = x2.shape[0]
    cos, sup, sdn = _rotary_tables(s_len)
    kern = functools.partial(_proj_kernel, tn=tn)
    return pl.pallas_call(
        kern,
        grid=(s_len // tm, PROJ_WIDTH // tn),
        in_specs=[
            pl.BlockSpec((tm, D_MODEL), lambda m, n: (m, 0)),
            pl.BlockSpec((1, D_MODEL), lambda m, n: (0, 0)),
            pl.BlockSpec((D_MODEL, tn), lambda m, n: (0, n)),
            pl.BlockSpec((tm, HEAD), lambda m, n: (m, 0)),
            pl.BlockSpec((tm, HEAD), lambda m, n: (m, 0)),
            pl.BlockSpec((tm, HEAD), lambda m, n: (m, 0)),
        ],
        out_specs=[
            pl.BlockSpec((tn // HEAD, tm, HEAD), lambda m, n: (n, m, 0)),
            pl.BlockSpec((N_HEADS, tm // DIFF_BQ, HEAD, DIFF_BQ), lambda m, n: (0, m, 0, 0)),
            pl.BlockSpec((N_HEADS, tm // CHUNK, HEAD, CHUNK), lambda m, n: (0, m, 0, 0)),
        ],
        out_shape=[
            jax.ShapeDtypeStruct((PROJ_WIDTH // HEAD, s_len, HEAD), BF16),
            jax.ShapeDtypeStruct((N_HEADS, s_len // DIFF_BQ, HEAD, DIFF_BQ), BF16),
            jax.ShapeDtypeStruct((N_HEADS, s_len // CHUNK, HEAD, CHUNK), BF16),
        ],
        scratch_shapes=[
            pltpu.VMEM((tm, D_MODEL), BF16),
            pltpu.VMEM((tm, tn), F32),
        ],
        compiler_params=_cparams(2),
        name="proj_in",
    )(x2, ln1.reshape(1, D_MODEL), w_in_bf16, cos, sup, sdn)


def _diff_kernel(lq1_ref, lk1_ref, lq2_ref, lk2_ref, qt_ref, qt_next_ref, k_ref, vt_ref, g_ref, o_ref,
                 qqt_ref, s0_ref, s1_ref, cmax0_ref, cmax1_ref, m_ref, acc_ref, first_slot_ref):
    bq, bk = DIFF_BQ, DIFF_BK
    n_chunks = bk // CHUNK
    i = pl.program_id(1)
    s_ref, cmax_ref = (s0_ref, s1_ref), (cmax0_ref, cmax1_ref)
    last = (i * bq) // bk
    ones = jnp.ones((ONES_ROWS, CHUNK), BF16)
    cur = i % 2

    def build_qqt(q_block_ref, buf):
        qt = q_block_ref[0, 0]
        sub = lax.broadcasted_iota(jnp.int32, (HEAD, bq), 0)
        zero = jnp.zeros_like(qt)
        qqt_ref[buf, :, 0:bq] = jnp.where(sub < QK_DIM, qt, zero)
        qqt_ref[buf, :, bq:2 * bq] = jnp.where(sub >= QK_DIM, qt, zero)

    def scores_chunk(j, slot, r, qbuf):
        rows = pl.ds(pl.multiple_of(j * bk + r * CHUNK, CHUNK), CHUNK)
        st = jnp.dot(k_ref[0, rows, :], qqt_ref[qbuf], preferred_element_type=F32)
        s_ref[slot][r * CHUNK:(r + 1) * CHUNK, :] = st
        cmax_ref[slot][r:r + 1, :] = jnp.max(st, axis=0, keepdims=True)

    def softmax_chunk(slot, r, m_next):
        return jnp.exp2((s_ref[slot][r * CHUNK:(r + 1) * CHUNK, :] - m_next).astype(BF16))

    def values_chunk(j, r, pt, pv):
        lhs = jnp.concatenate([vt_ref[0, j * n_chunks + r], ones], axis=0)
        d = jnp.dot(lhs, pt, preferred_element_type=F32)
        return d if pv is None else pv + d

    def stage(j, par, next_block, next_qbuf, used_chunks=n_chunks):
        m_prev = m_ref[...]
        m_next = jnp.maximum(m_prev, jnp.max(cmax_ref[par][0:used_chunks, :], axis=0, keepdims=True))
        alpha = jnp.exp2(m_prev - m_next)
        pv = None
        for r in range(n_chunks):
            scores_chunk(next_block, 1 - par, r, next_qbuf)
            if r < used_chunks:
                pv = values_chunk(j, r, softmax_chunk(par, r, m_next), pv)
        m_ref[...] = m_next
        acc_ref[...] = alpha * acc_ref[...] + pv

    @pl.when(i == 0)
    def _():
        build_qqt(qt_ref, 0)
        for r in range(n_chunks):
            scores_chunk(0, 0, r, 0)
        first_slot_ref[0] = 0

    m_ref[...] = jnp.full(m_ref.shape, NEG_INF, F32)
    acc_ref[...] = jnp.zeros(acc_ref.shape, F32)
    first_slot = first_slot_ref[0]

    def body(j, carry):
        @pl.when((first_slot + j) % 2 == 0)
        def _():
            stage(j, 0, j + 1, cur)

        @pl.when((first_slot + j) % 2 == 1)
        def _():
            stage(j, 1, j + 1, cur)

        return carry

    lax.fori_loop(0, last, body, 0)

    def drain(par, first_masked):
        used = first_masked + bq // CHUNK
        build_qqt(qt_next_ref, 1 - cur)
        for r in range(first_masked, used):
            st = s_ref[par][r * CHUNK:(r + 1) * CHUNK, :]
            key = r * CHUNK - first_masked * CHUNK + lax.broadcasted_iota(jnp.int32, st.shape, 0)
            col = lax.broadcasted_iota(jnp.int32, st.shape, 1)
            st = jnp.where(key <= jnp.where(col >= bq, col - bq, col), st, NEG_INF)
            s_ref[par][r * CHUNK:(r + 1) * CHUNK, :] = st
            cmax_ref[par][r:r + 1, :] = jnp.max(st, axis=0, keepdims=True)
        stage(last, par, 0, 1 - cur, used)
        first_slot_ref[0] = 1 - par

    q_per_k = bk // bq
    for par in range(2):
        for pos in range(q_per_k):
            @pl.when(jnp.logical_and((first_slot + last) % 2 == par, i % q_per_k == pos))
            def _(par=par, pos=pos):
                drain(par, pos * (bq // CHUNK))

    lam = (jnp.exp(jnp.sum(lq1_ref[...] * lk1_ref[...], axis=-1, keepdims=True))
           - jnp.exp(jnp.sum(lq2_ref[...] * lk2_ref[...], axis=-1, keepdims=True)) + LAM_INIT)
    ot = acc_ref[0:HEAD, :] / acc_ref[HEAD:HEAD + 1, :]
    dt = ot[:, 0:bq] - lam * ot[:, bq:2 * bq]
    yt = dt * lax.rsqrt(jnp.mean(dt * dt, axis=0, keepdims=True) + NORM_EPS)
    o_ref[...] = (yt.T * g_ref[...] * (1.0 - LAM_INIT)).astype(BF16)


def _diff_attention(proj, qt, vt, lq1, lk1, lq2, lk2, g):
    s_len = proj.shape[1]
    assert DIFF_BK % DIFF_BQ == 0 and s_len % DIFF_BK == 0
    vec = lambda a: a.reshape(1, -1).astype(F32)
    small = lambda width: pl.BlockSpec((1, width), lambda h, i: (0, 0))
    n_q = s_len // DIFF_BQ
    return pl.pallas_call(
        _diff_kernel,
        grid=(N_HEADS, s_len // DIFF_BQ),
        in_specs=[
            small(QK_DIM), small(QK_DIM), small(QK_DIM), small(QK_DIM),
            pl.BlockSpec((1, 1, HEAD, DIFF_BQ), lambda h, i: (h, i, 0, 0)),
            pl.BlockSpec((1, 1, HEAD, DIFF_BQ), lambda h, i: (h, jnp.minimum(i + 1, n_q - 1), 0, 0)),
            pl.BlockSpec((1, s_len, HEAD), lambda h, i: (N_HEADS + h, 0, 0)),
            pl.BlockSpec((1, s_len // CHUNK, HEAD, CHUNK), lambda h, i: (h, 0, 0, 0)),
            small(HEAD),
        ],
        out_specs=pl.BlockSpec((DIFF_BQ, HEAD), lambda h, i: (i, h)),
        out_shape=jax.ShapeDtypeStruct((s_len, N_HEADS * HEAD), BF16),
        scratch_shapes=[
            pltpu.VMEM((2, HEAD, 2 * DIFF_BQ), BF16),
            pltpu.VMEM((DIFF_BK, 2 * DIFF_BQ), F32),
            pltpu.VMEM((DIFF_BK, 2 * DIFF_BQ), F32),
            pltpu.VMEM((8, 2 * DIFF_BQ), F32),
            pltpu.VMEM((8, 2 * DIFF_BQ), F32),
            pltpu.VMEM((1, 2 * DIFF_BQ), F32),
            pltpu.VMEM((HEAD + ONES_ROWS, 2 * DIFF_BQ), F32),
            pltpu.SMEM((1,), jnp.int32),
        ],
        compiler_params=_cparams(2),
        name="diff_attn",
    )(vec(lq1), vec(lk1), vec(lq2), vec(lk2), qt, qt, proj, vt, vec(g))


def _sb_kernel(q_ref, k_ref, v_ref, g_ref, o_ref, acc_ref, stick_ref, *, blk, group):
    scale = HEAD ** -0.5
    r = lax.broadcasted_iota(jnp.int32, (blk, blk), 0)
    c = lax.broadcasted_iota(jnp.int32, (blk, blk), 1)
    later = jnp.where(r > c, 1.0, 0.0).astype(BF16)
    strict = c < r

    def logits(q, j):
        start = pl.multiple_of(j * blk, blk)
        return lax.dot_general(q, k_ref[0, pl.ds(start, blk), :], (((1,), (1,)), ((), ())),
                               preferred_element_type=F32) * scale

    def log_masses(z, masked):
        log_beta = jnp.minimum(z, 0.0) - jnp.log(1.0 + jnp.exp(-jnp.abs(z)))
        log_rest = log_beta - z
        if masked:
            log_rest = jnp.where(strict, log_rest, 0.0)
        hi = log_rest.astype(BF16)
        lo = (log_rest - hi.astype(F32)).astype(BF16)
        within = (jnp.dot(hi, later, preferred_element_type=F32)
                  + jnp.dot(lo, later, preferred_element_type=F32))
        return log_beta + within, jnp.sum(log_rest, axis=1, keepdims=True)

    def values(log_a, stick, j, masked):
        a = jnp.exp(log_a + stick)
        if masked:
            a = jnp.where(strict, a, 0.0)
        start = pl.multiple_of(j * blk, blk)
        return jnp.dot(a.astype(BF16), v_ref[0, pl.ds(start, blk), :], preferred_element_type=F32)

    def block(q, j, stick, masked):
        log_a, spent = log_masses(logits(q, j), masked)
        return values(log_a, stick, j, masked), spent

    first = pl.program_id(1) * group
    qs = [q_ref[0, g * blk:(g + 1) * blk, :] for g in range(group)]
    blocks = ([(g, first + g, True) for g in range(group)]
              + [(g, jnp.maximum(first + g - 1, 0), False) for g in range(group)])
    zs = [logits(qs[g], j) for g, j, _ in blocks]
    masses = [log_masses(z, masked) for z, (_, _, masked) in zip(zs, blocks)]
    for g in range(group):
        i = first + g
        (log_a_diag, spent_diag), (log_a_prev, spent_prev) = masses[g], masses[group + g]
        pv_diag = values(log_a_diag, 0.0, i, True)
        pv_prev = values(log_a_prev, spent_diag, jnp.maximum(i - 1, 0), False)
        has_prev = i > 0
        acc_ref[g] = pv_diag + jnp.where(has_prev, pv_prev, 0.0)
        stick_ref[g] = jnp.broadcast_to(spent_diag + jnp.where(has_prev, spent_prev, 0.0), stick_ref.shape[1:])

    for g in range(group):
        q = q_ref[0, g * blk:(g + 1) * blk, :]

        def alive():
            return jnp.max(stick_ref[g]) > SB_DEAD_LOG

        def cond(state):
            j, go = state
            return jnp.logical_and(j >= 0, go)

        def body(state):
            j, _ = state
            pv, spent = block(q, j, stick_ref[g, :, 0:1], False)
            acc_ref[g] += pv
            stick_ref[g] += spent
            return j - 1, alive()

        lax.while_loop(cond, body, (first + g - 2, alive()))
        o_ref[g * blk:(g + 1) * blk, :] = (_rms(acc_ref[g]) * g_ref[...]).astype(BF16)


def _sb_attention(proj, g, blk=256, group=4):
    s_len = proj.shape[1]
    rows = blk * group
    return pl.pallas_call(
        functools.partial(_sb_kernel, blk=blk, group=group),
        grid=(N_HEADS, s_len // rows),
        in_specs=[
            pl.BlockSpec((1, rows, HEAD), lambda h, i: (3 * N_HEADS + h, i, 0)),
            pl.BlockSpec((1, s_len, HEAD), lambda h, i: (4 * N_HEADS + h, 0, 0)),
            pl.BlockSpec((1, s_len, HEAD), lambda h, i: (5 * N_HEADS + h, 0, 0)),
            pl.BlockSpec((1, HEAD), lambda h, i: (0, 0)),
        ],
        out_specs=pl.BlockSpec((rows, HEAD), lambda h, i: (i, h)),
        out_shape=jax.ShapeDtypeStruct((s_len, N_HEADS * HEAD), BF16),
        scratch_shapes=[
            pltpu.VMEM((group, blk, HEAD), F32),
            pltpu.VMEM((group, blk, HEAD), F32),
        ],
        compiler_params=_cparams(2),
        name="sb_attn",
    )(proj, proj, proj, g.reshape(1, HEAD).astype(F32))


def _outproj_kernel(md_ref, ms_ref, x_ref, w_ref, g_ref, x1_ref, h2_ref):
    half = md_ref.shape[1]
    y = (jnp.dot(md_ref[...], w_ref[0:half, :], preferred_element_type=F32)
         + jnp.dot(ms_ref[...], w_ref[half:2 * half, :], preferred_element_type=F32))
    x1 = x_ref[...] + y
    x1_ref[...] = x1
    h2_ref[...] = (_rms(x1) * g_ref[...]).astype(BF16)


def _out_project(mixed_diff, mixed_sb, x2, w_out_bf16, ln2, tm=512):
    s_len = x2.shape[0]
    half = N_HEADS * HEAD
    return pl.pallas_call(
        _outproj_kernel,
        grid=(s_len // tm,),
        in_specs=[
            pl.BlockSpec((tm, half), lambda m: (m, 0)),
            pl.BlockSpec((tm, half), lambda m: (m, 0)),
            pl.BlockSpec((tm, D_MODEL), lambda m: (m, 0)),
            pl.BlockSpec((2 * half, D_MODEL), lambda m: (0, 0)),
            pl.BlockSpec((1, D_MODEL), lambda m: (0, 0)),
        ],
        out_specs=[
            pl.BlockSpec((tm, D_MODEL), lambda m: (m, 0)),
            pl.BlockSpec((tm, D_MODEL), lambda m: (m, 0)),
        ],
        out_shape=[
            jax.ShapeDtypeStruct((s_len, D_MODEL), F32),
            jax.ShapeDtypeStruct((s_len, D_MODEL), BF16),
        ],
        compiler_params=_cparams(1),
        name="proj_out",
    )(mixed_diff, mixed_sb, x2, w_out_bf16, ln2.reshape(1, D_MODEL))


def _mlp_kernel(h_ref, w1_ref, w2_ref, x1_ref, g_ref, o_ref, acc_ref):
    f = pl.program_id(1)

    @pl.when(f == 0)
    def _():
        acc_ref[...] = x1_ref[...]

    a = jnp.dot(h_ref[...], w1_ref[...], preferred_element_type=F32)
    a = jnp.square(jnp.maximum(a, 0.0)).astype(BF16)
    acc_ref[...] += jnp.dot(a, w2_ref[...], preferred_element_type=F32)

    @pl.when(f == pl.num_programs(1) - 1)
    def _():
        o_ref[...] = _rms(acc_ref[...]) * g_ref[...]


def _mlp(h2, x1, w1_bf16, w2_bf16, ln_f, tm=512, tf=1024):
    s_len = h2.shape[0]
    return pl.pallas_call(
        _mlp_kernel,
        grid=(s_len // tm, D_FF // tf),
        in_specs=[
            pl.BlockSpec((tm, D_MODEL), lambda m, f: (m, 0)),
            pl.BlockSpec((D_MODEL, tf), lambda m, f: (0, f)),
            pl.BlockSpec((tf, D_MODEL), lambda m, f: (f, 0)),
            pl.BlockSpec((tm, D_MODEL), lambda m, f: (m, 0)),
            pl.BlockSpec((1, D_MODEL), lambda m, f: (0, 0)),
        ],
        out_specs=pl.BlockSpec((tm, D_MODEL), lambda m, f: (m, 0)),
        out_shape=jax.ShapeDtypeStruct((s_len, D_MODEL), F32),
        scratch_shapes=[pltpu.VMEM((tm, D_MODEL), F32)],
        compiler_params=_cparams(2),
        name="mlp",
    )(h2, w1_bf16, w2_bf16, x1, ln_f.reshape(1, D_MODEL))


def kernel(x, ln1, w_in, lambda_q1, lambda_k1, lambda_q2, lambda_k2, diff_head_norm, sb_head_norm,
           w_out, ln2, w_mlp_in, w_mlp_out, ln_f):
    b, s_len, _ = x.shape
    assert b == 1 and ln1.shape[0] == 1
    x2 = x.reshape(s_len, D_MODEL)
    proj, qt, vt = _project(x2, ln1[0], w_in[0].astype(BF16))
    mixed_diff = _diff_attention(proj, qt, vt, lambda_q1[0], lambda_k1[0], lambda_q2[0], lambda_k2[0], diff_head_norm[0])
    mixed_sb = _sb_attention(proj, sb_head_norm[0])
    x1, h2 = _out_project(mixed_diff, mixed_sb, x2, w_out[0].astype(BF16), ln2[0])
    out = _mlp(h2, x1, w_mlp_in[0].astype(BF16), w_mlp_out[0].astype(BF16), ln_f)
    return out.reshape(b, s_len, D_MODEL)
```

```python
import functools
import math

import jax
import jax.numpy as jnp
import numpy as np
from jax import lax
from jax.experimental import pallas as pl
from jax.experimental.pallas import tpu as pltpu

F32 = jnp.float32
BF16 = jnp.bfloat16

D_MODEL = 2048
HEAD = 128
N_HEADS = 8
QK_DIM = 64
D_FF = 4 * D_MODEL
PROJ_WIDTH = 6 * N_HEADS * HEAD
ROT_COLS = 2 * N_HEADS * HEAD
ROT_DIMS = QK_DIM // 4
ROPE_THETA = 500000.0
NORM_EPS = 1e-6
NEG_INF = -1e30
LAM_INIT = 0.8 - 0.6 * math.exp(-0.3 * 0)
SB_DEAD_LOG = -105.0
DIFF_BQ = 512
DIFF_BK = 1024
ONES_ROWS = 16
CHUNK = 256
LOG2E = math.log2(math.e)
DIFF_QSCALE = QK_DIM ** -0.5 * LOG2E

VMEM_LIMIT = 56 * 1024 * 1024


def _cparams(n_axes, flags=None):
    return pltpu.CompilerParams(dimension_semantics=("arbitrary",) * n_axes,
                                vmem_limit_bytes=VMEM_LIMIT, flags=flags)


def _rms(x):
    return x * lax.rsqrt(jnp.mean(x * x, axis=-1, keepdims=True) + NORM_EPS)


def _proj_kernel(x_ref, g_ref, w_ref, cos_ref, sup_ref, sdn_ref, o_ref, qt_ref, vt_ref, h_ref, y_ref, *, tn):
    n = pl.program_id(1)
    n_heads_tile = tn // HEAD
    assert n_heads_tile == N_HEADS

    def project(with_rotary, with_norm=False):
        if with_norm:
            h_ref[...] = (_rms(x_ref[...]) * g_ref[...]).astype(BF16)
        acc = jnp.dot(h_ref[...], w_ref[...], preferred_element_type=F32)
        for c in range(n_heads_tile):
            y = acc[:, c * HEAD:(c + 1) * HEAD]
            if with_rotary:
                y = (y * cos_ref[...] + pltpu.roll(y, ROT_DIMS // 2, 1) * sup_ref[...]
                     + pltpu.roll(y, HEAD - ROT_DIMS // 2, 1) * sdn_ref[...])
            o_ref[c] = y.astype(BF16)
            y_ref[:, c * HEAD:(c + 1) * HEAD] = y

    @pl.when(n == 0)
    def _():
        project(True, with_norm=True)

    @pl.when(jnp.logical_and(n > 0, n < ROT_COLS // tn))
    def _():
        project(True)

    @pl.when(n >= ROT_COLS // tn)
    def _():
        project(False)

    def store_transposed(dst_ref, c, scale):
        blocks, _, width = dst_ref.shape[1:]
        for b in range(blocks):
            y = y_ref[b * width:(b + 1) * width, c * HEAD:(c + 1) * HEAD]
            dst_ref[c, b] = (y * scale).T.astype(BF16)

    @pl.when(n == 0)
    def _():
        for c in range(n_heads_tile):
            store_transposed(qt_ref, c, DIFF_QSCALE)

    @pl.when(n == 2)
    def _():
        for c in range(n_heads_tile):
            store_transposed(vt_ref, c, 1.0)


def _rotary_tables(s_len):
    half = ROT_DIMS // 2
    inv_freq = ROPE_THETA ** (-jnp.arange(0, ROT_DIMS, 2, dtype=F32) / ROT_DIMS)
    ang = inv_freq[:, None] * jnp.arange(s_len, dtype=F32)[None, :]
    trig = jnp.concatenate([jnp.cos(ang), jnp.sin(ang)], axis=0)
    comp = np.arange(HEAD) % QK_DIM
    first, second = comp < half, (comp >= half) & (comp < ROT_DIMS)
    freq = comp % half
    sel_cos, sel_sup, sel_sdn = (np.zeros((2 * half, HEAD), np.float32) for _ in range(3))
    lanes = np.arange(HEAD)
    sel_cos[freq[first | second], lanes[first | second]] = 1.0
    sel_sup[half + freq[second], lanes[second]] = 1.0
    sel_sdn[half + freq[first], lanes[first]] = -1.0
    spread = lambda sel: lax.dot_general(trig, jnp.asarray(sel), (((0,), (0,)), ((), ())),
                                         precision=lax.Precision.HIGHEST)
    return spread(sel_cos) + jnp.asarray((~(first | second)).astype(np.float32)), spread(sel_sup), spread(sel_sdn)


def _project(x2, ln1, w_in_bf16, tm=1024, tn=N_HEADS * HEAD):
    s_len = x2.shape[0]
    cos, sup, sdn = _rotary_tables(s_len)
    kern = functools.partial(_proj_kernel, tn=tn)
    return pl.pallas_call(
        kern,
        grid=(s_len // tm, PROJ_WIDTH // tn),
        in_specs=[
            pl.BlockSpec((tm, D_MODEL), lambda m, n: (m, 0)),
            pl.BlockSpec((1, D_MODEL), lambda m, n: (0, 0)),
            pl.BlockSpec((D_MODEL, tn), lambda m, n: (0, n)),
            pl.BlockSpec((tm, HEAD), lambda m, n: (m, 0)),
            pl.BlockSpec((tm, HEAD), lambda m, n: (m, 0)),
            pl.BlockSpec((tm, HEAD), lambda m, n: (m, 0)),
        ],
        out_specs=[
            pl.BlockSpec((tn // HEAD, tm, HEAD), lambda m, n: (n, m, 0)),
            pl.BlockSpec((N_HEADS, tm // DIFF_BQ, HEAD, DIFF_BQ), lambda m, n: (0, m, 0, 0)),
            pl.BlockSpec((N_HEADS, tm // CHUNK, HEAD, CHUNK), lambda m, n: (0, m, 0, 0)),
        ],
        out_shape=[
            jax.ShapeDtypeStruct((PROJ_WIDTH // HEAD, s_len, HEAD), BF16),
            jax.ShapeDtypeStruct((N_HEADS, s_len // DIFF_BQ, HEAD, DIFF_BQ), BF16),
            jax.ShapeDtypeStruct((N_HEADS, s_len // CHUNK, HEAD, CHUNK), BF16),
        ],
        scratch_shapes=[
            pltpu.VMEM((tm, D_MODEL), BF16),
            pltpu.VMEM((tm, tn), F32),
        ],
        compiler_params=_cparams(2),
        name="proj_in",
    )(x2, ln1.reshape(1, D_MODEL), w_in_bf16, cos, sup, sdn)


def _diff_kernel(lq1_ref, lk1_ref, lq2_ref, lk2_ref, qt_ref, qt_next_ref, k_ref, vt_ref, g_ref, o_ref,
                 qqt_ref, s0_ref, s1_ref, cmax0_ref, cmax1_ref, m_ref, acc_ref, first_slot_ref):
    bq, bk = DIFF_BQ, DIFF_BK
    n_chunks = bk // CHUNK
    i = pl.program_id(1)
    s_ref, cmax_ref = (s0_ref, s1_ref), (cmax0_ref, cmax1_ref)
    last = (i * bq) // bk
    ones = jnp.ones((ONES_ROWS, CHUNK), BF16)
    cur = i % 2

    def build_qqt(q_block_ref, buf):
        qt = q_block_ref[0, 0]
        sub = lax.broadcasted_iota(jnp.int32, (HEAD, bq), 0)
        zero = jnp.zeros_like(qt)
        qqt_ref[buf, :, 0:bq] = jnp.where(sub < QK_DIM, qt, zero)
        qqt_ref[buf, :, bq:2 * bq] = jnp.where(sub >= QK_DIM, qt, zero)

    def scores_chunk(j, slot, r, qbuf):
        rows = pl.ds(pl.multiple_of(j * bk + r * CHUNK, CHUNK), CHUNK)
        st = jnp.dot(k_ref[0, rows, :], qqt_ref[qbuf], preferred_element_type=F32)
        s_ref[slot][r * CHUNK:(r + 1) * CHUNK, :] = st
        cmax_ref[slot][r:r + 1, :] = jnp.max(st, axis=0, keepdims=True)

    def softmax_chunk(slot, r, m_next):
        return jnp.exp2((s_ref[slot][r * CHUNK:(r + 1) * CHUNK, :] - m_next).astype(BF16))

    def values_chunk(j, r, pt, pv):
        lhs = jnp.concatenate([vt_ref[0, j * n_chunks + r], ones], axis=0)
        d = jnp.dot(lhs, pt, preferred_element_type=F32)
        return d if pv is None else pv + d

    def stage(j, par, next_block, next_qbuf, used_chunks=n_chunks):
        m_prev = m_ref[...]
        m_next = jnp.maximum(m_prev, jnp.max(cmax_ref[par][0:used_chunks, :], axis=0, keepdims=True))
        alpha = jnp.exp2(m_prev - m_next)
        pv = None
        for r in range(n_chunks):
            scores_chunk(next_block, 1 - par, r, next_qbuf)
            if r < used_chunks:
                pv = values_chunk(j, r, softmax_chunk(par, r, m_next), pv)
        m_ref[...] = m_next
        acc_ref[...] = alpha * acc_ref[...] + pv

    @pl.when(i == 0)
    def _():
        build_qqt(qt_ref, 0)
        for r in range(n_chunks):
            scores_chunk(0, 0, r, 0)
        first_slot_ref[0] = 0

    m_ref[...] = jnp.full(m_ref.shape, NEG_INF, F32)
    acc_ref[...] = jnp.zeros(acc_ref.shape, F32)
    first_slot = first_slot_ref[0]

    def body(j, carry):
        @pl.when((first_slot + j) % 2 == 0)
        def _():
            stage(j, 0, j + 1, cur)

        @pl.when((first_slot + j) % 2 == 1)
        def _():
            stage(j, 1, j + 1, cur)

        return carry

    lax.fori_loop(0, last, body, 0)

    def drain(par, first_masked):
        used = first_masked + bq // CHUNK
        build_qqt(qt_next_ref, 1 - cur)
        for r in range(first_masked, used):
            st = s_ref[par][r * CHUNK:(r + 1) * CHUNK, :]
            key = r * CHUNK - first_masked * CHUNK + lax.broadcasted_iota(jnp.int32, st.shape, 0)
            col = lax.broadcasted_iota(jnp.int32, st.shape, 1)
            st = jnp.where(key <= jnp.where(col >= bq, col - bq, col), st, NEG_INF)
            s_ref[par][r * CHUNK:(r + 1) * CHUNK, :] = st
            cmax_ref[par][r:r + 1, :] = jnp.max(st, axis=0, keepdims=True)
        stage(last, par, 0, 1 - cur, used)
        first_slot_ref[0] = 1 - par

    q_per_k = bk // bq
    for par in range(2):
        for pos in range(q_per_k):
            @pl.when(jnp.logical_and((first_slot + last) % 2 == par, i % q_per_k == pos))
            def _(par=par, pos=pos):
                drain(par, pos * (bq // CHUNK))

    lam = (jnp.exp(jnp.sum(lq1_ref[...] * lk1_ref[...], axis=-1, keepdims=True))
           - jnp.exp(jnp.sum(lq2_ref[...] * lk2_ref[...], axis=-1, keepdims=True)) + LAM_INIT)
    ot = acc_ref[0:HEAD, :] / acc_ref[HEAD:HEAD + 1, :]
    dt = ot[:, 0:bq] - lam * ot[:, bq:2 * bq]
    yt = dt * lax.rsqrt(jnp.mean(dt * dt, axis=0, keepdims=True) + NORM_EPS)
    o_ref[...] = (yt.T * g_ref[...] * (1.0 - LAM_INIT)).astype(BF16)


def _diff_attention(proj, qt, vt, lq1, lk1, lq2, lk2, g):
    s_len = proj.shape[1]
    assert DIFF_BK % DIFF_BQ == 0 and s_len % DIFF_BK == 0
    vec = lambda a: a.reshape(1, -1).astype(F32)
    small = lambda width: pl.BlockSpec((1, width), lambda h, i: (0, 0))
    n_q = s_len // DIFF_BQ
    return pl.pallas_call(
        _diff_kernel,
        grid=(N_HEADS, s_len // DIFF_BQ),
        in_specs=[
            small(QK_DIM), small(QK_DIM), small(QK_DIM), small(QK_DIM),
            pl.BlockSpec((1, 1, HEAD, DIFF_BQ), lambda h, i: (h, i, 0, 0)),
            pl.BlockSpec((1, 1, HEAD, DIFF_BQ), lambda h, i: (h, jnp.minimum(i + 1, n_q - 1), 0, 0)),
            pl.BlockSpec((1, s_len, HEAD), lambda h, i: (N_HEADS + h, 0, 0)),
            pl.BlockSpec((1, s_len // CHUNK, HEAD, CHUNK), lambda h, i: (h, 0, 0, 0)),
            small(HEAD),
        ],
        out_specs=pl.BlockSpec((DIFF_BQ, HEAD), lambda h, i: (i, h)),
        out_shape=jax.ShapeDtypeStruct((s_len, N_HEADS * HEAD), BF16),
        scratch_shapes=[
            pltpu.VMEM((2, HEAD, 2 * DIFF_BQ), BF16),
            pltpu.VMEM((DIFF_BK, 2 * DIFF_BQ), F32),
            pltpu.VMEM((DIFF_BK, 2 * DIFF_BQ), F32),
            pltpu.VMEM((8, 2 * DIFF_BQ), F32),
            pltpu.VMEM((8, 2 * DIFF_BQ), F32),
            pltpu.VMEM((1, 2 * DIFF_BQ), F32),
            pltpu.VMEM((HEAD + ONES_ROWS, 2 * DIFF_BQ), F32),
            pltpu.SMEM((1,), jnp.int32),
        ],
        compiler_params=_cparams(2),
        name="diff_attn",
    )(vec(lq1), vec(lk1), vec(lq2), vec(lk2), qt, qt, proj, vt, vec(g))


def _sb_kernel(q_ref, k_ref, v_ref, g_ref, o_ref, acc_ref, stick_ref, *, blk, group):
    scale = HEAD ** -0.5
    r = lax.broadcasted_iota(jnp.int32, (blk, blk), 0)
    c = lax.broadcasted_iota(jnp.int32, (blk, blk), 1)
    later = jnp.where(r > c, 1.0, 0.0).astype(BF16)
    strict = c < r

    def logits(q, j):
        start = pl.multiple_of(j * blk, blk)
        return lax.dot_general(q, k_ref[0, pl.ds(start, blk), :], (((1,), (1,)), ((), ())),
                               preferred_element_type=F32) * scale

    def log_masses(z, masked):
        log_beta = jnp.minimum(z, 0.0) - jnp.log(1.0 + jnp.exp(-jnp.abs(z)))
        log_rest = log_beta - z
        if masked:
            log_rest = jnp.where(strict, log_rest, 0.0)
        hi = log_rest.astype(BF16)
        lo = (log_rest - hi.astype(F32)).astype(BF16)
        within = (jnp.dot(hi, later, preferred_element_type=F32)
                  + jnp.dot(lo, later, preferred_element_type=F32))
        return log_beta + within, jnp.sum(log_rest, axis=1, keepdims=True)

    def values(log_a, stick, j, masked):
        a = jnp.exp(log_a + stick)
        if masked:
            a = jnp.where(strict, a, 0.0)
        start = pl.multiple_of(j * blk, blk)
        return jnp.dot(a.astype(BF16), v_ref[0, pl.ds(start, blk), :], preferred_element_type=F32)

    def block(q, j, stick, masked):
        log_a, spent = log_masses(logits(q, j), masked)
        return values(log_a, stick, j, masked), spent

    first = pl.program_id(1) * group
    qs = [q_ref[0, g * blk:(g + 1) * blk, :] for g in range(group)]
    blocks = ([(g, first + g, True) for g in range(group)]
              + [(g, jnp.maximum(first + g - 1, 0), False) for g in range(group)])
    zs = [logits(qs[g], j) for g, j, _ in blocks]
    masses = [log_masses(z, masked) for z, (_, _, masked) in zip(zs, blocks)]
    for g in range(group):
        i = first + g
        (log_a_diag, spent_diag), (log_a_prev, spent_prev) = masses[g], masses[group + g]
        pv_diag = values(log_a_diag, 0.0, i, True)
        pv_prev = values(log_a_prev, spent_diag, jnp.maximum(i - 1, 0), False)
        has_prev = i > 0
        acc_ref[g] = pv_diag + jnp.where(has_prev, pv_prev, 0.0)
        stick_ref[g] = jnp.broadcast_to(spent_diag + jnp.where(has_prev, spent_prev, 0.0), stick_ref.shape[1:])

    for g in range(group):
        q = q_ref[0, g * blk:(g + 1) * blk, :]

        def alive():
            return jnp.max(stick_ref[g]) > SB_DEAD_LOG

        def cond(state):
            j, go = state
            return jnp.logical_and(j >= 0, go)

        def body(state):
            j, _ = state
            pv, spent = block(q, j, stick_ref[g, :, 0:1], False)
            acc_ref[g] += pv
            stick_ref[g] += spent
            return j - 1, alive()

        lax.while_loop(cond, body, (first + g - 2, alive()))
        o_ref[g * blk:(g + 1) * blk, :] = (_rms(acc_ref[g]) * g_ref[...]).astype(BF16)


def _sb_attention(proj, g, blk=256, group=4):
    s_len = proj.shape[1]
    rows = blk * group
    return pl.pallas_call(
        functools.partial(_sb_kernel, blk=blk, group=group),
        grid=(N_HEADS, s_len // rows),
        in_specs=[
            pl.BlockSpec((1, rows, HEAD), lambda h, i: (3 * N_HEADS + h, i, 0)),
            pl.BlockSpec((1, s_len, HEAD), lambda h, i: (4 * N_HEADS + h, 0, 0)),
            pl.BlockSpec((1, s_len, HEAD), lambda h, i: (5 * N_HEADS + h, 0, 0)),
            pl.BlockSpec((1, HEAD), lambda h, i: (0, 0)),
        ],
        out_specs=pl.BlockSpec((rows, HEAD), lambda h, i: (i, h)),
        out_shape=jax.ShapeDtypeStruct((s_len, N_HEADS * HEAD), BF16),
        scratch_shapes=[
            pltpu.VMEM((group, blk, HEAD), F32),
            pltpu.VMEM((group, blk, HEAD), F32),
        ],
        compiler_params=_cparams(2),
        name="sb_attn",
    )(proj, proj, proj, g.reshape(1, HEAD).astype(F32))


def _outproj_kernel(md_ref, ms_ref, x_ref, w_ref, g_ref, x1_ref, h2_ref):
    half = md_ref.shape[1]
    y = (jnp.dot(md_ref[...], w_ref[0:half, :], preferred_element_type=F32)
         + jnp.dot(ms_ref[...], w_ref[half:2 * half, :], preferred_element_type=F32))
    x1 = x_ref[...] + y
    x1_ref[...] = x1
    h2_ref[...] = (_rms(x1) * g_ref[...]).astype(BF16)


def _out_project(mixed_diff, mixed_sb, x2, w_out_bf16, ln2, tm=512):
    s_len = x2.shape[0]
    half = N_HEADS * HEAD
    return pl.pallas_call(
        _outproj_kernel,
        grid=(s_len // tm,),
        in_specs=[
            pl.BlockSpec((tm, half), lambda m: (m, 0)),
            pl.BlockSpec((tm, half), lambda m: (m, 0)),
            pl.BlockSpec((tm, D_MODEL), lambda m: (m, 0)),
            pl.BlockSpec((2 * half, D_MODEL), lambda m: (0, 0)),
            pl.BlockSpec((1, D_MODEL), lambda m: (0, 0)),
        ],
        out_specs=[
            pl.BlockSpec((tm, D_MODEL), lambda m: (m, 0)),
            pl.BlockSpec((tm, D_MODEL), lambda m: (m, 0)),
        ],
        out_shape=[
            jax.ShapeDtypeStruct((s_len, D_MODEL), F32),
            jax.ShapeDtypeStruct((s_len, D_MODEL), BF16),
        ],
        compiler_params=_cparams(1),
        name="proj_out",
    )(mixed_diff, mixed_sb, x2, w_out_bf16, ln2.reshape(1, D_MODEL))


def _mlp_kernel(h_ref, w1_ref, w2_ref, x1_ref, g_ref, o_ref, acc_ref):
    f = pl.program_id(1)

    @pl.when(f == 0)
    def _():
        acc_ref[...] = x1_ref[...]

    a = jnp.dot(h_ref[...], w1_ref[...], preferred_element_type=F32)
    a = jnp.square(jnp.maximum(a, 0.0)).astype(BF16)
    acc_ref[...] += jnp.dot(a, w2_ref[...], preferred_element_type=F32)

    @pl.when(f == pl.num_programs(1) - 1)
    def _():
        o_ref[...] = _rms(acc_ref[...]) * g_ref[...]


def _mlp(h2, x1, w1_bf16, w2_bf16, ln_f, tm=512, tf=1024):
    s_len = h2.shape[0]
    return pl.pallas_call(
        _mlp_kernel,
        grid=(s_len // tm, D_FF // tf),
        in_specs=[
            pl.BlockSpec((tm, D_MODEL), lambda m, f: (m, 0)),
            pl.BlockSpec((D_MODEL, tf), lambda m, f: (0, f)),
            pl.BlockSpec((tf, D_MODEL), lambda m, f: (f, 0)),
            pl.BlockSpec((tm, D_MODEL), lambda m, f: (m, 0)),
            pl.BlockSpec((1, D_MODEL), lambda m, f: (0, 0)),
        ],
        out_specs=pl.BlockSpec((tm, D_MODEL), lambda m, f: (m, 0)),
        out_shape=jax.ShapeDtypeStruct((s_len, D_MODEL), F32),
        scratch_shapes=[pltpu.VMEM((tm, D_MODEL), F32)],
        compiler_params=_cparams(2),
        name="mlp",
    )(h2, w1_bf16, w2_bf16, x1, ln_f.reshape(1, D_MODEL))


def kernel(x, ln1, w_in, lambda_q1, lambda_k1, lambda_q2, lambda_k2, diff_head_norm, sb_head_norm,
           w_out, ln2, w_mlp_in, w_mlp_out, ln_f):
    b, s_len, _ = x.shape
    assert b == 1 and ln1.shape[0] == 1
    x2 = x.reshape(s_len, D_MODEL)
    proj, qt, vt = _project(x2, ln1[0], w_in[0].astype(BF16))
    mixed_diff = _diff_attention(proj, qt, vt, lambda_q1[0], lambda_k1[0], lambda_q2[0], lambda_k2[0], diff_head_norm[0])
    mixed_sb = _sb_attention(proj, sb_head_norm[0])
    x1, h2 = _out_project(mixed_diff, mixed_sb, x2, w_out[0].astype(BF16), ln2[0])
    out = _mlp(h2, x1, w_mlp_in[0].astype(BF16), w_mlp_out[0].astype(BF16), ln_f)
    return out.reshape(b, s_len, D_MODEL)
```

```python
import functools
import math

import jax
import jax.numpy as jnp
import numpy as np
from jax import lax
from jax.experimental import pallas as pl
from jax.experimental.pallas import tpu as pltpu

F32 = jnp.float32
BF16 = jnp.bfloat16

D_MODEL = 2048
HEAD = 128
N_HEADS = 8
QK_DIM = 64
D_FF = 4 * D_MODEL
PROJ_WIDTH = 6 * N_HEADS * HEAD
ROT_COLS = 2 * N_HEADS * HEAD
ROT_DIMS = QK_DIM // 4
ROPE_THETA = 500000.0
NORM_EPS = 1e-6
NEG_INF = -1e30
LAM_INIT = 0.8 - 0.6 * math.exp(-0.3 * 0)
SB_DEAD_LOG = -105.0
DIFF_BQ = 512
DIFF_BK = 1024
ONES_ROWS = 16
CHUNK = 256
LOG2E = math.log2(math.e)
DIFF_QSCALE = QK_DIM ** -0.5 * LOG2E

VMEM_LIMIT = 56 * 1024 * 1024


def _cparams(n_axes, flags=None):
    return pltpu.CompilerParams(dimension_semantics=("arbitrary",) * n_axes,
                                vmem_limit_bytes=VMEM_LIMIT, flags=flags)


def _rms(x):
    return x * lax.rsqrt(jnp.mean(x * x, axis=-1, keepdims=True) + NORM_EPS)


def _proj_kernel(x_ref, g_ref, w_ref, cos_ref, sup_ref, sdn_ref, o_ref, qt_ref, vt_ref, h_ref, y_ref, *, tn):
    n = pl.program_id(1)
    n_heads_tile = tn // HEAD
    assert n_heads_tile == N_HEADS

    def project(with_rotary, with_norm=False):
        if with_norm:
            h_ref[...] = (_rms(x_ref[...]) * g_ref[...]).astype(BF16)
        acc = jnp.dot(h_ref[...], w_ref[...], preferred_element_type=F32)
        for c in range(n_heads_tile):
            y = acc[:, c * HEAD:(c + 1) * HEAD]
            if with_rotary:
                y = (y * cos_ref[...] + pltpu.roll(y, ROT_DIMS // 2, 1) * sup_ref[...]
                     + pltpu.roll(y, HEAD - ROT_DIMS // 2, 1) * sdn_ref[...])
            o_ref[c] = y.astype(BF16)
            y_ref[:, c * HEAD:(c + 1) * HEAD] = y

    @pl.when(n == 0)
    def _():
        project(True, with_norm=True)

    @pl.when(jnp.logical_and(n > 0, n < ROT_COLS // tn))
    def _():
        project(True)

    @pl.when(n >= ROT_COLS // tn)
    def _():
        project(False)

    def store_transposed(dst_ref, c, scale):
        blocks, _, width = dst_ref.shape[1:]
        for b in range(blocks):
            y = y_ref[b * width:(b + 1) * width, c * HEAD:(c + 1) * HEAD]
            dst_ref[c, b] = (y * scale).T.astype(BF16)

    @pl.when(n == 0)
    def _():
        for c in range(n_heads_tile):
            store_transposed(qt_ref, c, DIFF_QSCALE)

    @pl.when(n == 2)
    def _():
        for c in range(n_heads_tile):
            store_transposed(vt_ref, c, 1.0)


def _rotary_tables(s_len):
    half = ROT_DIMS // 2
    inv_freq = ROPE_THETA ** (-jnp.arange(0, ROT_DIMS, 2, dtype=F32) / ROT_DIMS)
    comp = np.arange(HEAD) % QK_DIM
    first, second = comp < half, (comp >= half) & (comp < ROT_DIMS)
    ang = jnp.arange(s_len, dtype=F32)[:, None] * inv_freq[comp % half][None, :]
    cos, sin = jnp.cos(ang), jnp.sin(ang)
    cos_t = jnp.where(first | second, cos, 1.0)
    sup_t = jnp.where(second, sin, 0.0)
    sdn_t = jnp.where(first, -sin, 0.0)
    return cos_t, sup_t, sdn_t


def _project(x2, ln1, w_in_bf16, tm=1024, tn=N_HEADS * HEAD):
    s_len = x2.shape[0]
    cos, sup, sdn = _rotary_tables(s_len)
    kern = functools.partial(_proj_kernel, tn=tn)
    return pl.pallas_call(
        kern,
        grid=(s_len // tm, PROJ_WIDTH // tn),
        in_specs=[
            pl.BlockSpec((tm, D_MODEL), lambda m, n: (m, 0)),
            pl.BlockSpec((1, D_MODEL), lambda m, n: (0, 0)),
            pl.BlockSpec((D_MODEL, tn), lambda m, n: (0, n)),
            pl.BlockSpec((tm, HEAD), lambda m, n: (m, 0)),
            pl.BlockSpec((tm, HEAD), lambda m, n: (m, 0)),
            pl.BlockSpec((tm, HEAD), lambda m, n: (m, 0)),
        ],
        out_specs=[
            pl.BlockSpec((tn // HEAD, tm, HEAD), lambda m, n: (n, m, 0)),
            pl.BlockSpec((N_HEADS, tm // DIFF_BQ, HEAD, DIFF_BQ), lambda m, n: (0, m, 0, 0)),
            pl.BlockSpec((N_HEADS, tm // CHUNK, HEAD, CHUNK), lambda m, n: (0, m, 0, 0)),
        ],
        out_shape=[
            jax.ShapeDtypeStruct((PROJ_WIDTH // HEAD, s_len, HEAD), BF16),
            jax.ShapeDtypeStruct((N_HEADS, s_len // DIFF_BQ, HEAD, DIFF_BQ), BF16),
            jax.ShapeDtypeStruct((N_HEADS, s_len // CHUNK, HEAD, CHUNK), BF16),
        ],
        scratch_shapes=[
            pltpu.VMEM((tm, D_MODEL), BF16),
            pltpu.VMEM((tm, tn), F32),
        ],
        compiler_params=_cparams(2),
        name="proj_in",
    )(x2, ln1.reshape(1, D_MODEL), w_in_bf16, cos, sup, sdn)


def _diff_kernel(lq1_ref, lk1_ref, lq2_ref, lk2_ref, qt_ref, qt_next_ref, k_ref, vt_ref, g_ref, o_ref,
                 qqt_ref, s0_ref, s1_ref, cmax0_ref, cmax1_ref, m_ref, acc_ref, first_slot_ref):
    bq, bk = DIFF_BQ, DIFF_BK
    n_chunks = bk // CHUNK
    i = pl.program_id(1)
    s_ref, cmax_ref = (s0_ref, s1_ref), (cmax0_ref, cmax1_ref)
    last = (i * bq) // bk
    ones = jnp.ones((ONES_ROWS, CHUNK), BF16)
    cur = i % 2

    def build_qqt(q_block_ref, buf):
        qt = q_block_ref[0, 0]
        sub = lax.broadcasted_iota(jnp.int32, (HEAD, bq), 0)
        zero = jnp.zeros_like(qt)
        qqt_ref[buf, :, 0:bq] = jnp.where(sub < QK_DIM, qt, zero)
        qqt_ref[buf, :, bq:2 * bq] = jnp.where(sub >= QK_DIM, qt, zero)

    def scores_chunk(j, slot, r, qbuf):
        rows = pl.ds(pl.multiple_of(j * bk + r * CHUNK, CHUNK), CHUNK)
        st = jnp.dot(k_ref[0, rows, :], qqt_ref[qbuf], preferred_element_type=F32)
        s_ref[slot][r * CHUNK:(r + 1) * CHUNK, :] = st
        cmax_ref[slot][r:r + 1, :] = jnp.max(st, axis=0, keepdims=True)

    def softmax_chunk(slot, r, m_next):
        return jnp.exp2((s_ref[slot][r * CHUNK:(r + 1) * CHUNK, :] - m_next).astype(BF16))

    def values_chunk(j, r, pt, pv):
        lhs = jnp.concatenate([vt_ref[0, j * n_chunks + r], ones], axis=0)
        d = jnp.dot(lhs, pt, preferred_element_type=F32)
        return d if pv is None else pv + d

    def stage(j, par, next_block, next_qbuf, used_chunks=n_chunks):
        m_prev = m_ref[...]
        m_next = jnp.maximum(m_prev, jnp.max(cmax_ref[par][0:used_chunks, :], axis=0, keepdims=True))
        alpha = jnp.exp2(m_prev - m_next)
        pv = None
        for r in range(n_chunks):
            scores_chunk(next_block, 1 - par, r, next_qbuf)
            if r < used_chunks:
                pv = values_chunk(j, r, softmax_chunk(par, r, m_next), pv)
        m_ref[...] = m_next
        acc_ref[...] = alpha * acc_ref[...] + pv

    @pl.when(i == 0)
    def _():
        build_qqt(qt_ref, 0)
        for r in range(n_chunks):
            scores_chunk(0, 0, r, 0)
        first_slot_ref[0] = 0

    m_ref[...] = jnp.full(m_ref.shape, NEG_INF, F32)
    acc_ref[...] = jnp.zeros(acc_ref.shape, F32)
    first_slot = first_slot_ref[0]

    def body(j, carry):
        @pl.when((first_slot + j) % 2 == 0)
        def _():
            stage(j, 0, j + 1, cur)

        @pl.when((first_slot + j) % 2 == 1)
        def _():
            stage(j, 1, j + 1, cur)

        return carry

    lax.fori_loop(0, last, body, 0)

    def drain(par, first_masked):
        used = first_masked + bq // CHUNK
        build_qqt(qt_next_ref, 1 - cur)
        for r in range(first_masked, used):
            st = s_ref[par][r * CHUNK:(r + 1) * CHUNK, :]
            key = r * CHUNK - first_masked * CHUNK + lax.broadcasted_iota(jnp.int32, st.shape, 0)
            col = lax.broadcasted_iota(jnp.int32, st.shape, 1)
            st = jnp.where(key <= jnp.where(col >= bq, col - bq, col), st, NEG_INF)
            s_ref[par][r * CHUNK:(r + 1) * CHUNK, :] = st
            cmax_ref[par][r:r + 1, :] = jnp.max(st, axis=0, keepdims=True)
        stage(last, par, 0, 1 - cur, used)
        first_slot_ref[0] = 1 - par

    q_per_k = bk // bq
    for par in range(2):
        for pos in range(q_per_k):
            @pl.when(jnp.logical_and((first_slot + last) % 2 == par, i % q_per_k == pos))
            def _(par=par, pos=pos):
                drain(par, pos * (bq // CHUNK))

    lam = (jnp.exp(jnp.sum(lq1_ref[...] * lk1_ref[...], axis=-1, keepdims=True))
           - jnp.exp(jnp.sum(lq2_ref[...] * lk2_ref[...], axis=-1, keepdims=True)) + LAM_INIT)
    ot = acc_ref[0:HEAD, :] / acc_ref[HEAD:HEAD + 1, :]
    dt = ot[:, 0:bq] - lam * ot[:, bq:2 * bq]
    yt = dt * lax.rsqrt(jnp.mean(dt * dt, axis=0, keepdims=True) + NORM_EPS)
    o_ref[...] = (yt.T * g_ref[...] * (1.0 - LAM_INIT)).astype(BF16)


def _diff_attention(proj, qt, vt, lq1, lk1, lq2, lk2, g):
    s_len = proj.shape[1]
    assert DIFF_BK % DIFF_BQ == 0 and s_len % DIFF_BK == 0
    vec = lambda a: a.reshape(1, -1).astype(F32)
    small = lambda width: pl.BlockSpec((1, width), lambda h, i: (0, 0))
    n_q = s_len // DIFF_BQ
    return pl.pallas_call(
        _diff_kernel,
        grid=(N_HEADS, s_len // DIFF_BQ),
        in_specs=[
            small(QK_DIM), small(QK_DIM), small(QK_DIM), small(QK_DIM),
            pl.BlockSpec((1, 1, HEAD, DIFF_BQ), lambda h, i: (h, i, 0, 0)),
            pl.BlockSpec((1, 1, HEAD, DIFF_BQ), lambda h, i: (h, jnp.minimum(i + 1, n_q - 1), 0, 0)),
            pl.BlockSpec((1, s_len, HEAD), lambda h, i: (N_HEADS + h, 0, 0)),
            pl.BlockSpec((1, s_len // CHUNK, HEAD, CHUNK), lambda h, i: (h, 0, 0, 0)),
            small(HEAD),
        ],
        out_specs=pl.BlockSpec((DIFF_BQ, HEAD), lambda h, i: (i, h)),
        out_shape=jax.ShapeDtypeStruct((s_len, N_HEADS * HEAD), BF16),
        scratch_shapes=[
            pltpu.VMEM((2, HEAD, 2 * DIFF_BQ), BF16),
            pltpu.VMEM((DIFF_BK, 2 * DIFF_BQ), F32),
            pltpu.VMEM((DIFF_BK, 2 * DIFF_BQ), F32),
            pltpu.VMEM((8, 2 * DIFF_BQ), F32),
            pltpu.VMEM((8, 2 * DIFF_BQ), F32),
            pltpu.VMEM((1, 2 * DIFF_BQ), F32),
            pltpu.VMEM((HEAD + ONES_ROWS, 2 * DIFF_BQ), F32),
            pltpu.SMEM((1,), jnp.int32),
        ],
        compiler_params=_cparams(2),
        name="diff_attn",
    )(vec(lq1), vec(lk1), vec(lq2), vec(lk2), qt, qt, proj, vt, vec(g))


def _sb_kernel(q_ref, k_ref, v_ref, g_ref, o_ref, acc_ref, stick_ref, *, blk, group):
    scale = HEAD ** -0.5
    r = lax.broadcasted_iota(jnp.int32, (blk, blk), 0)
    c = lax.broadcasted_iota(jnp.int32, (blk, blk), 1)
    later = jnp.where(r > c, 1.0, 0.0).astype(BF16)
    strict = c < r

    def logits(q, j):
        start = pl.multiple_of(j * blk, blk)
        return lax.dot_general(q, k_ref[0, pl.ds(start, blk), :], (((1,), (1,)), ((), ())),
                               preferred_element_type=F32) * scale

    def log_masses(z, masked):
        log_beta = jnp.minimum(z, 0.0) - jnp.log(1.0 + jnp.exp(-jnp.abs(z)))
        log_rest = log_beta - z
        if masked:
            log_rest = jnp.where(strict, log_rest, 0.0)
        hi = log_rest.astype(BF16)
        lo = (log_rest - hi.astype(F32)).astype(BF16)
        within = (jnp.dot(hi, later, preferred_element_type=F32)
                  + jnp.dot(lo, later, preferred_element_type=F32))
        return log_beta + within, jnp.sum(log_rest, axis=1, keepdims=True)

    def values(log_a, stick, j, masked):
        a = jnp.exp(log_a + stick)
        if masked:
            a = jnp.where(strict, a, 0.0)
        start = pl.multiple_of(j * blk, blk)
        return jnp.dot(a.astype(BF16), v_ref[0, pl.ds(start, blk), :], preferred_element_type=F32)

    def block(q, j, stick, masked):
        log_a, spent = log_masses(logits(q, j), masked)
        return values(log_a, stick, j, masked), spent

    first = pl.program_id(1) * group
    qs = [q_ref[0, g * blk:(g + 1) * blk, :] for g in range(group)]
    blocks = ([(g, first + g, True) for g in range(group)]
              + [(g, jnp.maximum(first + g - 1, 0), False) for g in range(group)])
    zs = [logits(qs[g], j) for g, j, _ in blocks]
    masses = [log_masses(z, masked) for z, (_, _, masked) in zip(zs, blocks)]
    for g in range(group):
        i = first + g
        (log_a_diag, spent_diag), (log_a_prev, spent_prev) = masses[g], masses[group + g]
        pv_diag = values(log_a_diag, 0.0, i, True)
        pv_prev = values(log_a_prev, spent_diag, jnp.maximum(i - 1, 0), False)
        has_prev = i > 0
        acc_ref[g] = pv_diag + jnp.where(has_prev, pv_prev, 0.0)
        stick_ref[g] = jnp.broadcast_to(spent_diag + jnp.where(has_prev, spent_prev, 0.0), stick_ref.shape[1:])

    for g in range(group):
        q = q_ref[0, g * blk:(g + 1) * blk, :]

        def alive():
            return jnp.max(stick_ref[g]) > SB_DEAD_LOG

        def cond(state):
            j, go = state
            return jnp.logical_and(j >= 0, go)

        def body(state):
            j, _ = state
            pv, spent = block(q, j, stick_ref[g, :, 0:1], False)
            acc_ref[g] += pv
            stick_ref[g] += spent
            return j - 1, alive()

        lax.while_loop(cond, body, (first + g - 2, alive()))
        o_ref[g * blk:(g + 1) * blk, :] = (_rms(acc_ref[g]) * g_ref[...]).astype(BF16)


def _sb_attention(proj, g, blk=256, group=4):
    s_len = proj.shape[1]
    rows = blk * group
    return pl.pallas_call(
        functools.partial(_sb_kernel, blk=blk, group=group),
        grid=(N_HEADS, s_len // rows),
        in_specs=[
            pl.BlockSpec((1, rows, HEAD), lambda h, i: (3 * N_HEADS + h, i, 0)),
            pl.BlockSpec((1, s_len, HEAD), lambda h, i: (4 * N_HEADS + h, 0, 0)),
            pl.BlockSpec((1, s_len, HEAD), lambda h, i: (5 * N_HEADS + h, 0, 0)),
            pl.BlockSpec((1, HEAD), lambda h, i: (0, 0)),
        ],
        out_specs=pl.BlockSpec((rows, HEAD), lambda h, i: (i, h)),
        out_shape=jax.ShapeDtypeStruct((s_len, N_HEADS * HEAD), BF16),
        scratch_shapes=[
            pltpu.VMEM((group, blk, HEAD), F32),
            pltpu.VMEM((group, blk, HEAD), F32),
        ],
        compiler_params=_cparams(2),
        name="sb_attn",
    )(proj, proj, proj, g.reshape(1, HEAD).astype(F32))


def _outproj_kernel(md_ref, ms_ref, x_ref, w_ref, g_ref, x1_ref, h2_ref):
    half = md_ref.shape[1]
    y = (jnp.dot(md_ref[...], w_ref[0:half, :], preferred_element_type=F32)
         + jnp.dot(ms_ref[...], w_ref[half:2 * half, :], preferred_element_type=F32))
    x1 = x_ref[...] + y
    x1_ref[...] = x1
    h2_ref[...] = (_rms(x1) * g_ref[...]).astype(BF16)


def _out_project(mixed_diff, mixed_sb, x2, w_out_bf16, ln2, tm=512):
    s_len = x2.shape[0]
    half = N_HEADS * HEAD
    return pl.pallas_call(
        _outproj_kernel,
        grid=(s_len // tm,),
        in_specs=[
            pl.BlockSpec((tm, half), lambda m: (m, 0)),
            pl.BlockSpec((tm, half), lambda m: (m, 0)),
            pl.BlockSpec((tm, D_MODEL), lambda m: (m, 0)),
            pl.BlockSpec((2 * half, D_MODEL), lambda m: (0, 0)),
            pl.BlockSpec((1, D_MODEL), lambda m: (0, 0)),
        ],
        out_specs=[
            pl.BlockSpec((tm, D_MODEL), lambda m: (m, 0)),
            pl.BlockSpec((tm, D_MODEL), lambda m: (m, 0)),
        ],
        out_shape=[
            jax.ShapeDtypeStruct((s_len, D_MODEL), F32),
            jax.ShapeDtypeStruct((s_len, D_MODEL), BF16),
        ],
        compiler_params=_cparams(1),
        name="proj_out",
    )(mixed_diff, mixed_sb, x2, w_out_bf16, ln2.reshape(1, D_MODEL))


def _mlp_kernel(h_ref, w1_ref, w2_ref, x1_ref, g_ref, o_ref, acc_ref):
    f = pl.program_id(1)

    @pl.when(f == 0)
    def _():
        acc_ref[...] = x1_ref[...]

    a = jnp.dot(h_ref[...], w1_ref[...], preferred_element_type=F32)
    a = jnp.square(jnp.maximum(a, 0.0)).astype(BF16)
    acc_ref[...] += jnp.dot(a, w2_ref[...], preferred_element_type=F32)

    @pl.when(f == pl.num_programs(1) - 1)
    def _():
        o_ref[...] = _rms(acc_ref[...]) * g_ref[...]


def _mlp(h2, x1, w1_bf16, w2_bf16, ln_f, tm=512, tf=1024):
    s_len = h2.shape[0]
    return pl.pallas_call(
        _mlp_kernel,
        grid=(s_len // tm, D_FF // tf),
        in_specs=[
            pl.BlockSpec((tm, D_MODEL), lambda m, f: (m, 0)),
            pl.BlockSpec((D_MODEL, tf), lambda m, f: (0, f)),
            pl.BlockSpec((tf, D_MODEL), lambda m, f: (f, 0)),
            pl.BlockSpec((tm, D_MODEL), lambda m, f: (m, 0)),
            pl.BlockSpec((1, D_MODEL), lambda m, f: (0, 0)),
        ],
        out_specs=pl.BlockSpec((tm, D_MODEL), lambda m, f: (m, 0)),
        out_shape=jax.ShapeDtypeStruct((s_len, D_MODEL), F32),
        scratch_shapes=[pltpu.VMEM((tm, D_MODEL), F32)],
        compiler_params=_cparams(2),
        name="mlp",
    )(h2, w1_bf16, w2_bf16, x1, ln_f.reshape(1, D_MODEL))


def kernel(x, ln1, w_in, lambda_q1, lambda_k1, lambda_q2, lambda_k2, diff_head_norm, sb_head_norm,
           w_out, ln2, w_mlp_in, w_mlp_out, ln_f):
    b, s_len, _ = x.shape
    assert b == 1 and ln1.shape[0] == 1
    x2 = x.reshape(s_len, D_MODEL)
    proj, qt, vt = _project(x2, ln1[0], w_in[0].astype(BF16))
    mixed_diff = _diff_attention(proj, qt, vt, lambda_q1[0], lambda_k1[0], lambda_q2[0], lambda_k2[0], diff_head_norm[0])
    mixed_sb = _sb_attention(proj, sb_head_norm[0])
    x1, h2 = _out_project(mixed_diff, mixed_sb, x2, w_out[0].astype(BF16), ln2[0])
    out = _mlp(h2, x1, w_mlp_in[0].astype(BF16), w_mlp_out[0].astype(BF16), ln_f)
    return out.reshape(b, s_len, D_MODEL)
```

```python
import functools
import math

import jax
import jax.numpy as jnp
import numpy as np
from jax import lax
from jax.experimental import pallas as pl
from jax.experimental.pallas import tpu as pltpu

F32 = jnp.float32
BF16 = jnp.bfloat16

D_MODEL = 2048
HEAD = 128
N_HEADS = 8
QK_DIM = 64
D_FF = 4 * D_MODEL
PROJ_WIDTH = 6 * N_HEADS * HEAD
ROT_COLS = 2 * N_HEADS * HEAD
ROT_DIMS = QK_DIM // 4
ROPE_THETA = 500000.0
NORM_EPS = 1e-6
NEG_INF = -1e30
LAM_INIT = 0.8 - 0.6 * math.exp(-0.3 * 0)
SB_DEAD_LOG = -105.0
DIFF_BQ = 512
DIFF_BK = 1024
ONES_ROWS = 16
CHUNK = 256
LOG2E = math.log2(math.e)
DIFF_QSCALE = QK_DIM ** -0.5 * LOG2E

VMEM_LIMIT = 56 * 1024 * 1024


def _cparams(n_axes, flags=None):
    return pltpu.CompilerParams(dimension_semantics=("arbitrary",) * n_axes,
                                vmem_limit_bytes=VMEM_LIMIT, flags=flags)


def _rms(x):
    return x * lax.rsqrt(jnp.mean(x * x, axis=-1, keepdims=True) + NORM_EPS)


def _proj_kernel(x_ref, g_ref, w_ref, cos_ref, sup_ref, sdn_ref, o_ref, qt_ref, vt_ref, h_ref, y_ref, *, tn):
    n = pl.program_id(1)
    n_heads_tile = tn // HEAD
    assert n_heads_tile == N_HEADS

    def project(with_rotary, with_norm=False):
        if with_norm:
            h_ref[...] = (_rms(x_ref[...]) * g_ref[...]).astype(BF16)
        acc = jnp.dot(h_ref[...], w_ref[...], preferred_element_type=F32)
        for c in range(n_heads_tile):
            y = acc[:, c * HEAD:(c + 1) * HEAD]
            if with_rotary:
                y = (y * cos_ref[...] + pltpu.roll(y, ROT_DIMS // 2, 1) * sup_ref[...]
                     + pltpu.roll(y, HEAD - ROT_DIMS // 2, 1) * sdn_ref[...])
            o_ref[c] = y.astype(BF16)
            y_ref[:, c * HEAD:(c + 1) * HEAD] = y

    @pl.when(n == 0)
    def _():
        project(True, with_norm=True)

    @pl.when(jnp.logical_and(n > 0, n < ROT_COLS // tn))
    def _():
        project(True)

    @pl.when(n >= ROT_COLS // tn)
    def _():
        project(False)

    def store_transposed(dst_ref, c, scale):
        blocks, _, width = dst_ref.shape[1:]
        for b in range(blocks):
            y = y_ref[b * width:(b + 1) * width, c * HEAD:(c + 1) * HEAD]
            dst_ref[c, b] = (y * scale).T.astype(BF16)

    @pl.when(n == 0)
    def _():
        for c in range(n_heads_tile):
            store_transposed(qt_ref, c, DIFF_QSCALE)

    @pl.when(n == 2)
    def _():
        for c in range(n_heads_tile):
            store_transposed(vt_ref, c, 1.0)


def _rotary_tables(s_len):
    half = ROT_DIMS // 2
    inv_freq = ROPE_THETA ** (-jnp.arange(0, ROT_DIMS, 2, dtype=F32) / ROT_DIMS)
    comp = np.arange(HEAD) % QK_DIM
    first, second = comp < half, (comp >= half) & (comp < ROT_DIMS)
    ang = jnp.arange(s_len, dtype=F32)[:, None] * inv_freq[comp % half][None, :]
    cos, sin = jnp.cos(ang), jnp.sin(ang)
    cos_t = jnp.where(first | second, cos, 1.0)
    sup_t = jnp.where(second, sin, 0.0)
    sdn_t = jnp.where(first, -sin, 0.0)
    return cos_t, sup_t, sdn_t


def _project(x2, ln1, w_in_bf16, tm=1024, tn=N_HEADS * HEAD):
    s_len = x2.shape[0]
    cos, sup, sdn = _rotary_tables(s_len)
    kern = functools.partial(_proj_kernel, tn=tn)
    return pl.pallas_call(
        kern,
        grid=(s_len // tm, PROJ_WIDTH // tn),
        in_specs=[
            pl.BlockSpec((tm, D_MODEL), lambda m, n: (m, 0)),
            pl.BlockSpec((1, D_MODEL), lambda m, n: (0, 0)),
            pl.BlockSpec((D_MODEL, tn), lambda m, n: (0, n)),
            pl.BlockSpec((tm, HEAD), lambda m, n: (m, 0)),
            pl.BlockSpec((tm, HEAD), lambda m, n: (m, 0)),
            pl.BlockSpec((tm, HEAD), lambda m, n: (m, 0)),
        ],
        out_specs=[
            pl.BlockSpec((tn // HEAD, tm, HEAD), lambda m, n: (n, m, 0)),
            pl.BlockSpec((N_HEADS, tm // DIFF_BQ, HEAD, DIFF_BQ), lambda m, n: (0, m, 0, 0)),
            pl.BlockSpec((N_HEADS, tm // CHUNK, HEAD, CHUNK), lambda m, n: (0, m, 0, 0)),
        ],
        out_shape=[
            jax.ShapeDtypeStruct((PROJ_WIDTH // HEAD, s_len, HEAD), BF16),
            jax.ShapeDtypeStruct((N_HEADS, s_len // DIFF_BQ, HEAD, DIFF_BQ), BF16),
            jax.ShapeDtypeStruct((N_HEADS, s_len // CHUNK, HEAD, CHUNK), BF16),
        ],
        scratch_shapes=[
            pltpu.VMEM((tm, D_MODEL), BF16),
            pltpu.VMEM((tm, tn), F32),
        ],
        compiler_params=_cparams(2),
        name="proj_in",
    )(x2, ln1.reshape(1, D_MODEL), w_in_bf16, cos, sup, sdn)


def _diff_kernel(lq1_ref, lk1_ref, lq2_ref, lk2_ref, qt_ref, qt_next_ref, k_ref, vt_ref, g_ref, o_ref,
                 qqt_ref, s0_ref, s1_ref, cmax0_ref, cmax1_ref, m_ref, acc_ref, first_slot_ref):
    bq, bk = DIFF_BQ, DIFF_BK
    n_chunks = bk // CHUNK
    i = pl.program_id(1)
    s_ref, cmax_ref = (s0_ref, s1_ref), (cmax0_ref, cmax1_ref)
    last = (i * bq) // bk
    ones = jnp.ones((ONES_ROWS, CHUNK), BF16)
    cur = i % 2

    def build_qqt(q_block_ref, buf):
        qt = q_block_ref[0, 0]
        sub = lax.broadcasted_iota(jnp.int32, (HEAD, bq), 0)
        zero = jnp.zeros_like(qt)
        qqt_ref[buf, :, 0:bq] = jnp.where(sub < QK_DIM, qt, zero)
        qqt_ref[buf, :, bq:2 * bq] = jnp.where(sub >= QK_DIM, qt, zero)

    def scores_chunk(j, slot, r, qbuf):
        rows = pl.ds(pl.multiple_of(j * bk + r * CHUNK, CHUNK), CHUNK)
        st = jnp.dot(k_ref[0, rows, :], qqt_ref[qbuf], preferred_element_type=F32)
        s_ref[slot][r * CHUNK:(r + 1) * CHUNK, :] = st
        cmax_ref[slot][r:r + 1, :] = jnp.max(st, axis=0, keepdims=True)

    def softmax_chunk(slot, r, m_next):
        return jnp.exp2((s_ref[slot][r * CHUNK:(r + 1) * CHUNK, :] - m_next).astype(BF16))

    def values_chunk(j, r, pt, pv):
        lhs = jnp.concatenate([vt_ref[0, j * n_chunks + r], ones], axis=0)
        d = jnp.dot(lhs, pt, preferred_element_type=F32)
        return d if pv is None else pv + d

    def stage(j, par, next_block, next_qbuf, used_chunks=n_chunks):
        m_prev = m_ref[...]
        m_next = jnp.maximum(m_prev, jnp.max(cmax_ref[par][0:used_chunks, :], axis=0, keepdims=True))
        alpha = jnp.exp2(m_prev - m_next)
        pv = None
        for r in range(n_chunks):
            scores_chunk(next_block, 1 - par, r, next_qbuf)
            if r < used_chunks:
                pv = values_chunk(j, r, softmax_chunk(par, r, m_next), pv)
        m_ref[...] = m_next
        acc_ref[...] = alpha * acc_ref[...] + pv

    @pl.when(i == 0)
    def _():
        build_qqt(qt_ref, 0)
        for r in range(n_chunks):
            scores_chunk(0, 0, r, 0)
        first_slot_ref[0] = 0

    m_ref[...] = jnp.full(m_ref.shape, NEG_INF, F32)
    acc_ref[...] = jnp.zeros(acc_ref.shape, F32)
    first_slot = first_slot_ref[0]

    def body(j, carry):
        @pl.when((first_slot + j) % 2 == 0)
        def _():
            stage(j, 0, j + 1, cur)

        @pl.when((first_slot + j) % 2 == 1)
        def _():
            stage(j, 1, j + 1, cur)

        return carry

    lax.fori_loop(0, last, body, 0)

    def drain(par, first_masked):
        used = first_masked + bq // CHUNK
        build_qqt(qt_next_ref, 1 - cur)
        for r in range(first_masked, used):
            st = s_ref[par][r * CHUNK:(r + 1) * CHUNK, :]
            key = r * CHUNK - first_masked * CHUNK + lax.broadcasted_iota(jnp.int32, st.shape, 0)
            col = lax.broadcasted_iota(jnp.int32, st.shape, 1)
            st = jnp.where(key <= jnp.where(col >= bq, col - bq, col), st, NEG_INF)
            s_ref[par][r * CHUNK:(r + 1) * CHUNK, :] = st
            cmax_ref[par][r:r + 1, :] = jnp.max(st, axis=0, keepdims=True)
        stage(last, par, 0, 1 - cur, used)
        first_slot_ref[0] = 1 - par

    q_per_k = bk // bq
    for par in range(2):
        for pos in range(q_per_k):
            @pl.when(jnp.logical_and((first_slot + last) % 2 == par, i % q_per_k == pos))
            def _(par=par, pos=pos):
                drain(par, pos * (bq // CHUNK))

    lam = (jnp.exp(jnp.sum(lq1_ref[...] * lk1_ref[...], axis=-1, keepdims=True))
           - jnp.exp(jnp.sum(lq2_ref[...] * lk2_ref[...], axis=-1, keepdims=True)) + LAM_INIT)
    ot = acc_ref[0:HEAD, :] / acc_ref[HEAD:HEAD + 1, :]
    dt = ot[:, 0:bq] - lam * ot[:, bq:2 * bq]
    yt = dt * lax.rsqrt(jnp.mean(dt * dt, axis=0, keepdims=True) + NORM_EPS)
    o_ref[...] = (yt.T * g_ref[...] * (1.0 - LAM_INIT)).astype(BF16)


def _diff_attention(proj, qt, vt, lq1, lk1, lq2, lk2, g):
    s_len = proj.shape[1]
    assert DIFF_BK % DIFF_BQ == 0 and s_len % DIFF_BK == 0
    vec = lambda a: a.reshape(1, -1).astype(F32)
    small = lambda width: pl.BlockSpec((1, width), lambda h, i: (0, 0))
    n_q = s_len // DIFF_BQ
    return pl.pallas_call(
        _diff_kernel,
        grid=(N_HEADS, s_len // DIFF_BQ),
        in_specs=[
            small(QK_DIM), small(QK_DIM), small(QK_DIM), small(QK_DIM),
            pl.BlockSpec((1, 1, HEAD, DIFF_BQ), lambda h, i: (h, i, 0, 0)),
            pl.BlockSpec((1, 1, HEAD, DIFF_BQ), lambda h, i: (h, jnp.minimum(i + 1, n_q - 1), 0, 0)),
            pl.BlockSpec((1, s_len, HEAD), lambda h, i: (N_HEADS + h, 0, 0)),
            pl.BlockSpec((1, s_len // CHUNK, HEAD, CHUNK), lambda h, i: (h, 0, 0, 0)),
            small(HEAD),
        ],
        out_specs=pl.BlockSpec((DIFF_BQ, HEAD), lambda h, i: (i, h)),
        out_shape=jax.ShapeDtypeStruct((s_len, N_HEADS * HEAD), BF16),
        scratch_shapes=[
            pltpu.VMEM((2, HEAD, 2 * DIFF_BQ), BF16),
            pltpu.VMEM((DIFF_BK, 2 * DIFF_BQ), F32),
            pltpu.VMEM((DIFF_BK, 2 * DIFF_BQ), F32),
            pltpu.VMEM((8, 2 * DIFF_BQ), F32),
            pltpu.VMEM((8, 2 * DIFF_BQ), F32),
            pltpu.VMEM((1, 2 * DIFF_BQ), F32),
            pltpu.VMEM((HEAD + ONES_ROWS, 2 * DIFF_BQ), F32),
            pltpu.SMEM((1,), jnp.int32),
        ],
        compiler_params=_cparams(2),
        name="diff_attn",
    )(vec(lq1), vec(lk1), vec(lq2), vec(lk2), qt, qt, proj, vt, vec(g))


def _sb_kernel(q_ref, k_ref, v_ref, g_ref, o_ref, acc_ref, stick_ref, *, blk, group):
    scale = HEAD ** -0.5
    r = lax.broadcasted_iota(jnp.int32, (blk, blk), 0)
    c = lax.broadcasted_iota(jnp.int32, (blk, blk), 1)
    later = jnp.where(r > c, 1.0, 0.0).astype(BF16)
    strict = c < r

    def logits(q, j):
        start = pl.multiple_of(j * blk, blk)
        return lax.dot_general(q, k_ref[0, pl.ds(start, blk), :], (((1,), (1,)), ((), ())),
                               preferred_element_type=F32) * scale

    def log_masses(z, masked):
        log_beta = jnp.minimum(z, 0.0) - jnp.log(1.0 + jnp.exp(-jnp.abs(z)))
        log_rest = log_beta - z
        if masked:
            log_rest = jnp.where(strict, log_rest, 0.0)
        hi = log_rest.astype(BF16)
        lo = (log_rest - hi.astype(F32)).astype(BF16)
        within = (jnp.dot(hi, later, preferred_element_type=F32)
                  + jnp.dot(lo, later, preferred_element_type=F32))
        return log_beta + within, jnp.sum(log_rest, axis=1, keepdims=True)

    def values(log_a, stick, j, masked):
        a = jnp.exp(log_a + stick)
        if masked:
            a = jnp.where(strict, a, 0.0)
        start = pl.multiple_of(j * blk, blk)
        return jnp.dot(a.astype(BF16), v_ref[0, pl.ds(start, blk), :], preferred_element_type=F32)

    def block(q, j, stick, masked):
        log_a, spent = log_masses(logits(q, j), masked)
        return values(log_a, stick, j, masked), spent

    first = pl.program_id(1) * group
    qs = [q_ref[0, g * blk:(g + 1) * blk, :] for g in range(group)]
    blocks = ([(g, first + g, True) for g in range(group)]
              + [(g, jnp.maximum(first + g - 1, 0), False) for g in range(group)])
    zs = [logits(qs[g], j) for g, j, _ in blocks]
    masses = [log_masses(z, masked) for z, (_, _, masked) in zip(zs, blocks)]
    for g in range(group):
        i = first + g
        (log_a_diag, spent_diag), (log_a_prev, spent_prev) = masses[g], masses[group + g]
        pv_diag = values(log_a_diag, 0.0, i, True)
        pv_prev = values(log_a_prev, spent_diag, jnp.maximum(i - 1, 0), False)
        has_prev = i > 0
        acc_ref[g] = pv_diag + jnp.where(has_prev, pv_prev, 0.0)
        stick_ref[g] = jnp.broadcast_to(spent_diag + jnp.where(has_prev, spent_prev, 0.0), stick_ref.shape[1:])

    for g in range(group):
        q = q_ref[0, g * blk:(g + 1) * blk, :]

        def alive():
            return jnp.max(stick_ref[g]) > SB_DEAD_LOG

        def cond(state):
            j, go = state
            return jnp.logical_and(j >= 0, go)

        def body(state):
            j, _ = state
            pv, spent = block(q, j, stick_ref[g, :, 0:1], False)
            acc_ref[g] += pv
            stick_ref[g] += spent
            return j - 1, alive()

        lax.while_loop(cond, body, (first + g - 2, alive()))
        o_ref[g * blk:(g + 1) * blk, :] = (_rms(acc_ref[g]) * g_ref[...]).astype(BF16)


def _sb_attention(proj, g, blk=256, group=4):
    s_len = proj.shape[1]
    rows = blk * group
    return pl.pallas_call(
        functools.partial(_sb_kernel, blk=blk, group=group),
        grid=(N_HEADS, s_len // rows),
        in_specs=[
            pl.BlockSpec((1, rows, HEAD), lambda h, i: (3 * N_HEADS + h, i, 0)),
            pl.BlockSpec((1, s_len, HEAD), lambda h, i: (4 * N_HEADS + h, 0, 0)),
            pl.BlockSpec((1, s_len, HEAD), lambda h, i: (5 * N_HEADS + h, 0, 0)),
            pl.BlockSpec((1, HEAD), lambda h, i: (0, 0)),
        ],
        out_specs=pl.BlockSpec((rows, HEAD), lambda h, i: (i, h)),
        out_shape=jax.ShapeDtypeStruct((s_len, N_HEADS * HEAD), BF16),
        scratch_shapes=[
            pltpu.VMEM((group, blk, HEAD), F32),
            pltpu.VMEM((group, blk, HEAD), F32),
        ],
        compiler_params=_cparams(2),
        name="sb_attn",
    )(proj, proj, proj, g.reshape(1, HEAD).astype(F32))


def _outproj_kernel(md_ref, ms_ref, x_ref, w_ref, g_ref, x1_ref, h2_ref):
    half = md_ref.shape[1]
    y = (jnp.dot(md_ref[...], w_ref[0:half, :], preferred_element_type=F32)
         + jnp.dot(ms_ref[...], w_ref[half:2 * half, :], preferred_element_type=F32))
    x1 = x_ref[...] + y
    x1_ref[...] = x1
    h2_ref[...] = (_rms(x1) * g_ref[...]).astype(BF16)


def _out_project(mixed_diff, mixed_sb, x2, w_out_bf16, ln2, tm=512):
    s_len = x2.shape[0]
    half = N_HEADS * HEAD
    return pl.pallas_call(
        _outproj_kernel,
        grid=(s_len // tm,),
        in_specs=[
            pl.BlockSpec((tm, half), lambda m: (m, 0)),
            pl.BlockSpec((tm, half), lambda m: (m, 0)),
            pl.BlockSpec((tm, D_MODEL), lambda m: (m, 0)),
            pl.BlockSpec((2 * half, D_MODEL), lambda m: (0, 0)),
            pl.BlockSpec((1, D_MODEL), lambda m: (0, 0)),
        ],
        out_specs=[
            pl.BlockSpec((tm, D_MODEL), lambda m: (m, 0)),
            pl.BlockSpec((tm, D_MODEL), lambda m: (m, 0)),
        ],
        out_shape=[
            jax.ShapeDtypeStruct((s_len, D_MODEL), F32),
            jax.ShapeDtypeStruct((s_len, D_MODEL), BF16),
        ],
        compiler_params=_cparams(1),
        name="proj_out",
    )(mixed_diff, mixed_sb, x2, w_out_bf16, ln2.reshape(1, D_MODEL))


def _mlp_kernel(h_ref, w1_ref, w2_ref, x1_ref, g_ref, o_ref, acc_ref):
    f = pl.program_id(1)

    @pl.when(jnp.logical_and(pl.program_id(0) == 0, f == 0))
    def _():
        acc_ref[...] = jnp.zeros(acc_ref.shape, F32)

    a = jnp.dot(h_ref[...], w1_ref[...], preferred_element_type=F32)
    a = jnp.square(jnp.maximum(a, 0.0)).astype(BF16)
    base = jnp.where(f == 0, x1_ref[...], acc_ref[...])
    acc_ref[...] = base + jnp.dot(a, w2_ref[...], preferred_element_type=F32)

    @pl.when(f == pl.num_programs(1) - 1)
    def _():
        o_ref[...] = _rms(acc_ref[...]) * g_ref[...]


def _mlp(h2, x1, w1_bf16, w2_bf16, ln_f, tm=512, tf=1024):
    s_len = h2.shape[0]
    return pl.pallas_call(
        _mlp_kernel,
        grid=(s_len // tm, D_FF // tf),
        in_specs=[
            pl.BlockSpec((tm, D_MODEL), lambda m, f: (m, 0)),
            pl.BlockSpec((D_MODEL, tf), lambda m, f: (0, f)),
            pl.BlockSpec((tf, D_MODEL), lambda m, f: (f, 0)),
            pl.BlockSpec((tm, D_MODEL), lambda m, f: (m, 0)),
            pl.BlockSpec((1, D_MODEL), lambda m, f: (0, 0)),
        ],
        out_specs=pl.BlockSpec((tm, D_MODEL), lambda m, f: (m, 0)),
        out_shape=jax.ShapeDtypeStruct((s_len, D_MODEL), F32),
        scratch_shapes=[pltpu.VMEM((tm, D_MODEL), F32)],
        compiler_params=_cparams(2),
        name="mlp",
    )(h2, w1_bf16, w2_bf16, x1, ln_f.reshape(1, D_MODEL))


def kernel(x, ln1, w_in, lambda_q1, lambda_k1, lambda_q2, lambda_k2, diff_head_norm, sb_head_norm,
           w_out, ln2, w_mlp_in, w_mlp_out, ln_f):
    b, s_len, _ = x.shape
    assert b == 1 and ln1.shape[0] == 1
    x2 = x.reshape(s_len, D_MODEL)
    proj, qt, vt = _project(x2, ln1[0], w_in[0].astype(BF16))
    mixed_diff = _diff_attention(proj, qt, vt, lambda_q1[0], lambda_k1[0], lambda_q2[0], lambda_k2[0], diff_head_norm[0])
    mixed_sb = _sb_attention(proj, sb_head_norm[0])
    x1, h2 = _out_project(mixed_diff, mixed_sb, x2, w_out[0].astype(BF16), ln2[0])
    out = _mlp(h2, x1, w_mlp_in[0].astype(BF16), w_mlp_out[0].astype(BF16), ln_f)
    return out.reshape(b, s_len, D_MODEL)
```

```python
import functools
import math

import jax
import jax.numpy as jnp
import numpy as np
from jax import lax
from jax.experimental import pallas as pl
from jax.experimental.pallas import tpu as pltpu

F32 = jnp.float32
BF16 = jnp.bfloat16

D_MODEL = 2048
HEAD = 128
N_HEADS = 8
QK_DIM = 64
D_FF = 4 * D_MODEL
PROJ_WIDTH = 6 * N_HEADS * HEAD
ROT_COLS = 2 * N_HEADS * HEAD
ROT_DIMS = QK_DIM // 4
ROPE_THETA = 500000.0
NORM_EPS = 1e-6
NEG_INF = -1e30
LAM_INIT = 0.8 - 0.6 * math.exp(-0.3 * 0)
SB_DEAD_LOG = -105.0
DIFF_BQ = 512
DIFF_BK = 1024
ONES_ROWS = 16
CHUNK = 256
LOG2E = math.log2(math.e)
DIFF_QSCALE = QK_DIM ** -0.5 * LOG2E

VMEM_LIMIT = 56 * 1024 * 1024


def _cparams(n_axes, flags=None):
    return pltpu.CompilerParams(dimension_semantics=("arbitrary",) * n_axes,
                                vmem_limit_bytes=VMEM_LIMIT, flags=flags)


def _rms(x):
    return x * lax.rsqrt(jnp.mean(x * x, axis=-1, keepdims=True) + NORM_EPS)


def _proj_kernel(x_ref, g_ref, w_ref, cos_ref, sup_ref, sdn_ref, o_ref, qt_ref, vt_ref, h_ref, y_ref, *, tn):
    n = pl.program_id(1)
    n_heads_tile = tn // HEAD
    assert n_heads_tile == N_HEADS

    def project(with_rotary, with_norm=False):
        if with_norm:
            h_ref[...] = (_rms(x_ref[...]) * g_ref[...]).astype(BF16)
        acc = jnp.dot(h_ref[...], w_ref[...], preferred_element_type=F32)
        for c in range(n_heads_tile):
            y = acc[:, c * HEAD:(c + 1) * HEAD]
            if with_rotary:
                y = (y * cos_ref[...] + pltpu.roll(y, ROT_DIMS // 2, 1) * sup_ref[...]
                     + pltpu.roll(y, HEAD - ROT_DIMS // 2, 1) * sdn_ref[...])
            o_ref[c] = y.astype(BF16)
            y_ref[:, c * HEAD:(c + 1) * HEAD] = y

    @pl.when(n == 0)
    def _():
        project(True, with_norm=True)

    @pl.when(jnp.logical_and(n > 0, n < ROT_COLS // tn))
    def _():
        project(True)

    @pl.when(n >= ROT_COLS // tn)
    def _():
        project(False)

    def store_transposed(dst_ref, c, scale):
        blocks, _, width = dst_ref.shape[1:]
        for b in range(blocks):
            y = y_ref[b * width:(b + 1) * width, c * HEAD:(c + 1) * HEAD]
            dst_ref[c, b] = (y * scale).T.astype(BF16)

    @pl.when(n == 0)
    def _():
        for c in range(n_heads_tile):
            store_transposed(qt_ref, c, DIFF_QSCALE)

    @pl.when(n == 2)
    def _():
        for c in range(n_heads_tile):
            store_transposed(vt_ref, c, 1.0)


def _rotary_tables(s_len):
    half = ROT_DIMS // 2
    inv_freq = ROPE_THETA ** (-jnp.arange(0, ROT_DIMS, 2, dtype=F32) / ROT_DIMS)
    comp = np.arange(HEAD) % QK_DIM
    first, second = comp < half, (comp >= half) & (comp < ROT_DIMS)
    ang = jnp.arange(s_len, dtype=F32)[:, None] * inv_freq[comp % half][None, :]
    cos, sin = jnp.cos(ang), jnp.sin(ang)
    cos_t = jnp.where(first | second, cos, 1.0)
    sup_t = jnp.where(second, sin, 0.0)
    sdn_t = jnp.where(first, -sin, 0.0)
    return cos_t, sup_t, sdn_t


def _project(x2, ln1, w_in_bf16, tm=1024, tn=N_HEADS * HEAD):
    s_len = x2.shape[0]
    cos, sup, sdn = _rotary_tables(s_len)
    kern = functools.partial(_proj_kernel, tn=tn)
    return pl.pallas_call(
        kern,
        grid=(s_len // tm, PROJ_WIDTH // tn),
        in_specs=[
            pl.BlockSpec((tm, D_MODEL), lambda m, n: (m, 0)),
            pl.BlockSpec((1, D_MODEL), lambda m, n: (0, 0)),
            pl.BlockSpec((D_MODEL, tn), lambda m, n: (0, n)),
            pl.BlockSpec((tm, HEAD), lambda m, n: (m, 0)),
            pl.BlockSpec((tm, HEAD), lambda m, n: (m, 0)),
            pl.BlockSpec((tm, HEAD), lambda m, n: (m, 0)),
        ],
        out_specs=[
            pl.BlockSpec((tn // HEAD, tm, HEAD), lambda m, n: (n, m, 0)),
            pl.BlockSpec((N_HEADS, tm // DIFF_BQ, HEAD, DIFF_BQ), lambda m, n: (0, m, 0, 0)),
            pl.BlockSpec((N_HEADS, tm // CHUNK, HEAD, CHUNK), lambda m, n: (0, m, 0, 0)),
        ],
        out_shape=[
            jax.ShapeDtypeStruct((PROJ_WIDTH // HEAD, s_len, HEAD), BF16),
            jax.ShapeDtypeStruct((N_HEADS, s_len // DIFF_BQ, HEAD, DIFF_BQ), BF16),
            jax.ShapeDtypeStruct((N_HEADS, s_len // CHUNK, HEAD, CHUNK), BF16),
        ],
        scratch_shapes=[
            pltpu.VMEM((tm, D_MODEL), BF16),
            pltpu.VMEM((tm, tn), F32),
        ],
        compiler_params=_cparams(2),
        name="proj_in",
    )(x2, ln1.reshape(1, D_MODEL), w_in_bf16, cos, sup, sdn)


def _diff_kernel(lq1_ref, lk1_ref, lq2_ref, lk2_ref, qt_ref, qt_next_ref, k_ref, vt_ref, g_ref, o_ref,
                 qqt_ref, s0_ref, s1_ref, cmax0_ref, cmax1_ref, m_ref, acc_ref, first_slot_ref):
    bq, bk = DIFF_BQ, DIFF_BK
    n_chunks = bk // CHUNK
    i = pl.program_id(1)
    s_ref, cmax_ref = (s0_ref, s1_ref), (cmax0_ref, cmax1_ref)
    last = (i * bq) // bk
    ones = jnp.ones((ONES_ROWS, CHUNK), BF16)
    cur = i % 2

    def build_qqt(q_block_ref, buf):
        qt = q_block_ref[0, 0]
        sub = lax.broadcasted_iota(jnp.int32, (HEAD, bq), 0)
        zero = jnp.zeros_like(qt)
        qqt_ref[buf, :, 0:bq] = jnp.where(sub < QK_DIM, qt, zero)
        qqt_ref[buf, :, bq:2 * bq] = jnp.where(sub >= QK_DIM, qt, zero)

    def scores_chunk(j, slot, r, qbuf):
        rows = pl.ds(pl.multiple_of(j * bk + r * CHUNK, CHUNK), CHUNK)
        st = jnp.dot(k_ref[0, rows, :], qqt_ref[qbuf], preferred_element_type=F32)
        s_ref[slot][r * CHUNK:(r + 1) * CHUNK, :] = st
        cmax_ref[slot][r:r + 1, :] = jnp.max(st, axis=0, keepdims=True)

    def softmax_chunk(slot, r, m_next):
        return jnp.exp2((s_ref[slot][r * CHUNK:(r + 1) * CHUNK, :] - m_next).astype(BF16))

    def values_chunk(j, r, pt, pv):
        lhs = jnp.concatenate([vt_ref[0, j * n_chunks + r], ones], axis=0)
        d = jnp.dot(lhs, pt, preferred_element_type=F32)
        return d if pv is None else pv + d

    def stage(j, par, next_block, next_qbuf, used_chunks=n_chunks):
        m_prev = m_ref[...]
        m_next = jnp.maximum(m_prev, jnp.max(cmax_ref[par][0:used_chunks, :], axis=0, keepdims=True))
        alpha = jnp.exp2(m_prev - m_next)
        pv = None
        for r in range(n_chunks):
            scores_chunk(next_block, 1 - par, r, next_qbuf)
            if r < used_chunks:
                pv = values_chunk(j, r, softmax_chunk(par, r, m_next), pv)
        m_ref[...] = m_next
        acc_ref[...] = alpha * acc_ref[...] + pv

    @pl.when(i == 0)
    def _():
        build_qqt(qt_ref, 0)
        for r in range(n_chunks):
            scores_chunk(0, 0, r, 0)
        first_slot_ref[0] = 0

    m_ref[...] = jnp.full(m_ref.shape, NEG_INF, F32)
    acc_ref[...] = jnp.zeros(acc_ref.shape, F32)
    first_slot = first_slot_ref[0]

    def body(j, carry):
        @pl.when((first_slot + j) % 2 == 0)
        def _():
            stage(j, 0, j + 1, cur)

        @pl.when((first_slot + j) % 2 == 1)
        def _():
            stage(j, 1, j + 1, cur)

        return carry

    lax.fori_loop(0, last, body, 0)

    def drain(par, first_masked):
        used = first_masked + bq // CHUNK
        build_qqt(qt_next_ref, 1 - cur)
        for r in range(first_masked, used):
            st = s_ref[par][r * CHUNK:(r + 1) * CHUNK, :]
            key = r * CHUNK - first_masked * CHUNK + lax.broadcasted_iota(jnp.int32, st.shape, 0)
            col = lax.broadcasted_iota(jnp.int32, st.shape, 1)
            st = jnp.where(key <= jnp.where(col >= bq, col - bq, col), st, NEG_INF)
            s_ref[par][r * CHUNK:(r + 1) * CHUNK, :] = st
            cmax_ref[par][r:r + 1, :] = jnp.max(st, axis=0, keepdims=True)
        stage(last, par, 0, 1 - cur, used)
        first_slot_ref[0] = 1 - par

    q_per_k = bk // bq
    for par in range(2):
        for pos in range(q_per_k):
            @pl.when(jnp.logical_and((first_slot + last) % 2 == par, i % q_per_k == pos))
            def _(par=par, pos=pos):
                drain(par, pos * (bq // CHUNK))

    lam = (jnp.exp(jnp.sum(lq1_ref[...] * lk1_ref[...], axis=-1, keepdims=True))
           - jnp.exp(jnp.sum(lq2_ref[...] * lk2_ref[...], axis=-1, keepdims=True)) + LAM_INIT)
    ot = acc_ref[0:HEAD, :] / acc_ref[HEAD:HEAD + 1, :]
    dt = ot[:, 0:bq] - lam * ot[:, bq:2 * bq]
    yt = dt * lax.rsqrt(jnp.mean(dt * dt, axis=0, keepdims=True) + NORM_EPS)
    o_ref[...] = (yt.T * g_ref[...] * (1.0 - LAM_INIT)).astype(BF16)


def _diff_attention(proj, qt, vt, lq1, lk1, lq2, lk2, g):
    s_len = proj.shape[1]
    assert DIFF_BK % DIFF_BQ == 0 and s_len % DIFF_BK == 0
    vec = lambda a: a.reshape(1, -1).astype(F32)
    small = lambda width: pl.BlockSpec((1, width), lambda h, i: (0, 0))
    n_q = s_len // DIFF_BQ
    return pl.pallas_call(
        _diff_kernel,
        grid=(N_HEADS, s_len // DIFF_BQ),
        in_specs=[
            small(QK_DIM), small(QK_DIM), small(QK_DIM), small(QK_DIM),
            pl.BlockSpec((1, 1, HEAD, DIFF_BQ), lambda h, i: (h, i, 0, 0)),
            pl.BlockSpec((1, 1, HEAD, DIFF_BQ), lambda h, i: (h, jnp.minimum(i + 1, n_q - 1), 0, 0)),
            pl.BlockSpec((1, s_len, HEAD), lambda h, i: (N_HEADS + h, 0, 0)),
            pl.BlockSpec((1, s_len // CHUNK, HEAD, CHUNK), lambda h, i: (h, 0, 0, 0)),
            small(HEAD),
        ],
        out_specs=pl.BlockSpec((DIFF_BQ, HEAD), lambda h, i: (i, h)),
        out_shape=jax.ShapeDtypeStruct((s_len, N_HEADS * HEAD), BF16),
        scratch_shapes=[
            pltpu.VMEM((2, HEAD, 2 * DIFF_BQ), BF16),
            pltpu.VMEM((DIFF_BK, 2 * DIFF_BQ), F32),
            pltpu.VMEM((DIFF_BK, 2 * DIFF_BQ), F32),
            pltpu.VMEM((8, 2 * DIFF_BQ), F32),
            pltpu.VMEM((8, 2 * DIFF_BQ), F32),
            pltpu.VMEM((1, 2 * DIFF_BQ), F32),
            pltpu.VMEM((HEAD + ONES_ROWS, 2 * DIFF_BQ), F32),
            pltpu.SMEM((1,), jnp.int32),
        ],
        compiler_params=_cparams(2),
        name="diff_attn",
    )(vec(lq1), vec(lk1), vec(lq2), vec(lk2), qt, qt, proj, vt, vec(g))


def _sb_kernel(q_ref, k_ref, v_ref, g_ref, o_ref, acc_ref, stick_ref, *, blk, group):
    scale = HEAD ** -0.5
    r = lax.broadcasted_iota(jnp.int32, (blk, blk), 0)
    c = lax.broadcasted_iota(jnp.int32, (blk, blk), 1)
    later = jnp.where(r > c, 1.0, 0.0).astype(BF16)
    strict = c < r

    def logits(q, j):
        start = pl.multiple_of(j * blk, blk)
        return lax.dot_general(q, k_ref[0, pl.ds(start, blk), :], (((1,), (1,)), ((), ())),
                               preferred_element_type=F32) * scale

    def log_masses(z, masked):
        log_beta = jnp.minimum(z, 0.0) - jnp.log(1.0 + jnp.exp(-jnp.abs(z)))
        log_rest = log_beta - z
        if masked:
            log_rest = jnp.where(strict, log_rest, 0.0)
        hi = log_rest.astype(BF16)
        lo = (log_rest - hi.astype(F32)).astype(BF16)
        within = (jnp.dot(hi, later, preferred_element_type=F32)
                  + jnp.dot(lo, later, preferred_element_type=F32))
        return log_beta + within, jnp.sum(log_rest, axis=1, keepdims=True)

    def values(log_a, stick, j, masked):
        a = jnp.exp(log_a + stick)
        if masked:
            a = jnp.where(strict, a, 0.0)
        start = pl.multiple_of(j * blk, blk)
        return jnp.dot(a.astype(BF16), v_ref[0, pl.ds(start, blk), :], preferred_element_type=F32)

    def block(q, j, stick, masked):
        log_a, spent = log_masses(logits(q, j), masked)
        return values(log_a, stick, j, masked), spent

    first = pl.program_id(1) * group
    qs = [q_ref[0, g * blk:(g + 1) * blk, :] for g in range(group)]
    blocks = ([(g, first + g, True) for g in range(group)]
              + [(g, jnp.maximum(first + g - 1, 0), False) for g in range(group)])
    zs = [logits(qs[g], j) for g, j, _ in blocks]
    masses = [log_masses(z, masked) for z, (_, _, masked) in zip(zs, blocks)]
    for g in range(group):
        i = first + g
        (log_a_diag, spent_diag), (log_a_prev, spent_prev) = masses[g], masses[group + g]
        pv_diag = values(log_a_diag, 0.0, i, True)
        pv_prev = values(log_a_prev, spent_diag, jnp.maximum(i - 1, 0), False)
        has_prev = i > 0
        acc_ref[g] = pv_diag + jnp.where(has_prev, pv_prev, 0.0)
        stick_ref[g] = jnp.broadcast_to(spent_diag + jnp.where(has_prev, spent_prev, 0.0), stick_ref.shape[1:])

    for g in range(group):
        q = q_ref[0, g * blk:(g + 1) * blk, :]

        def alive():
            return jnp.max(stick_ref[g]) > SB_DEAD_LOG

        def cond(state):
            j, go = state
            return jnp.logical_and(j >= 0, go)

        def body(state):
            j, _ = state
            pv, spent = block(q, j, stick_ref[g, :, 0:1], False)
            acc_ref[g] += pv
            stick_ref[g] += spent
            return j - 1, alive()

        lax.while_loop(cond, body, (first + g - 2, alive()))
        o_ref[g * blk:(g + 1) * blk, :] = (_rms(acc_ref[g]) * g_ref[...]).astype(BF16)


def _sb_attention(proj, g, blk=256, group=4):
    s_len = proj.shape[1]
    rows = blk * group
    return pl.pallas_call(
        functools.partial(_sb_kernel, blk=blk, group=group),
        grid=(N_HEADS, s_len // rows),
        in_specs=[
            pl.BlockSpec((1, rows, HEAD), lambda h, i: (3 * N_HEADS + h, i, 0)),
            pl.BlockSpec((1, s_len, HEAD), lambda h, i: (4 * N_HEADS + h, 0, 0)),
            pl.BlockSpec((1, s_len, HEAD), lambda h, i: (5 * N_HEADS + h, 0, 0)),
            pl.BlockSpec((1, HEAD), lambda h, i: (0, 0)),
        ],
        out_specs=pl.BlockSpec((rows, HEAD), lambda h, i: (i, h)),
        out_shape=jax.ShapeDtypeStruct((s_len, N_HEADS * HEAD), BF16),
        scratch_shapes=[
            pltpu.VMEM((group, blk, HEAD), F32),
            pltpu.VMEM((group, blk, HEAD), F32),
        ],
        compiler_params=_cparams(2),
        name="sb_attn",
    )(proj, proj, proj, g.reshape(1, HEAD).astype(F32))


def _out_mlp_kernel(md_ref, ms_ref, x_ref, wo_ref, g2_ref, w1_ref, w2_ref, gf_ref, o_ref, acc_ref, h_ref):
    f = pl.program_id(1)

    @pl.when(f == 0)
    def _():
        half = md_ref.shape[1]
        y = (jnp.dot(md_ref[...], wo_ref[0:half, :], preferred_element_type=F32)
             + jnp.dot(ms_ref[...], wo_ref[half:2 * half, :], preferred_element_type=F32))
        x1 = x_ref[...] + y
        acc_ref[...] = x1
        h_ref[...] = (_rms(x1) * g2_ref[...]).astype(BF16)

    a = jnp.dot(h_ref[...], w1_ref[...], preferred_element_type=F32)
    a = jnp.square(jnp.maximum(a, 0.0)).astype(BF16)
    acc_ref[...] += jnp.dot(a, w2_ref[...], preferred_element_type=F32)

    @pl.when(f == pl.num_programs(1) - 1)
    def _():
        o_ref[...] = _rms(acc_ref[...]) * gf_ref[...]


def _out_mlp(mixed_diff, mixed_sb, x2, w_out_bf16, ln2, w1_bf16, w2_bf16, ln_f, tm=512, tf=1024):
    s_len = x2.shape[0]
    half = N_HEADS * HEAD
    row = lambda width: pl.BlockSpec((1, width), lambda m, f: (0, 0))
    return pl.pallas_call(
        _out_mlp_kernel,
        grid=(s_len // tm, D_FF // tf),
        in_specs=[
            pl.BlockSpec((tm, half), lambda m, f: (m, 0)),
            pl.BlockSpec((tm, half), lambda m, f: (m, 0)),
            pl.BlockSpec((tm, D_MODEL), lambda m, f: (m, 0)),
            pl.BlockSpec((2 * half, D_MODEL), lambda m, f: (0, 0), pipeline_mode=pl.Buffered(1)),
            row(D_MODEL),
            pl.BlockSpec((D_MODEL, tf), lambda m, f: (0, f)),
            pl.BlockSpec((tf, D_MODEL), lambda m, f: (f, 0)),
            row(D_MODEL),
        ],
        out_specs=pl.BlockSpec((tm, D_MODEL), lambda m, f: (m, 0)),
        out_shape=jax.ShapeDtypeStruct((s_len, D_MODEL), F32),
        scratch_shapes=[
            pltpu.VMEM((tm, D_MODEL), F32),
            pltpu.VMEM((tm, D_MODEL), BF16),
        ],
        compiler_params=_cparams(2),
        name="out_mlp",
    )(mixed_diff, mixed_sb, x2, w_out_bf16, ln2.reshape(1, D_MODEL), w1_bf16, w2_bf16, ln_f.reshape(1, D_MODEL))


def kernel(x, ln1, w_in, lambda_q1, lambda_k1, lambda_q2, lambda_k2, diff_head_norm, sb_head_norm,
           w_out, ln2, w_mlp_in, w_mlp_out, ln_f):
    b, s_len, _ = x.shape
    assert b == 1 and ln1.shape[0] == 1
    x2 = x.reshape(s_len, D_MODEL)
    proj, qt, vt = _project(x2, ln1[0], w_in[0].astype(BF16))
    mixed_diff = _diff_attention(proj, qt, vt, lambda_q1[0], lambda_k1[0], lambda_q2[0], lambda_k2[0], diff_head_norm[0])
    mixed_sb = _sb_attention(proj, sb_head_norm[0])
    out = _out_mlp(mixed_diff, mixed_sb, x2, w_out[0].astype(BF16), ln2[0],
                   w_mlp_in[0].astype(BF16), w_mlp_out[0].astype(BF16), ln_f)
    return out.reshape(b, s_len, D_MODEL)
```
